```python
import math
import jax, jax.numpy as jnp
from jax import lax
import numpy as np

D_MODEL = 1024
BATCH = 2
SEQ = 8192
DEPTH = 1

CHUNK = 64
N_META = 16
QBLK = 128
HEAD_DIM = 64
N_DIFF_HEADS = D_MODEL // 256
DIFF_V_DIM = 2 * HEAD_DIM
DIFF_QK = N_DIFF_HEADS * 2 * HEAD_DIM
DIFF_WIDTH = N_DIFF_HEADS * DIFF_V_DIM
N_SB_HEADS = D_MODEL // 128
SB_WIDTH = N_SB_HEADS * HEAD_DIM
MIX_WIDTH = DIFF_WIDTH + SB_WIDTH
PROJ_SIZES = (DIFF_QK, DIFF_QK, DIFF_WIDTH, SB_WIDTH, SB_WIDTH, SB_WIDTH)
PROJ_WIDTH = sum(PROJ_SIZES)
N_BUCKETS = 32
MAX_DIST = 128
N_GROUPS = 4
EXPERTS_PER_GROUP = 4
N_EXPERTS = N_GROUPS * EXPERTS_PER_GROUP
EXPERT_TOP_K = 2
D_EXPERT = D_MODEL // 2
EPS = 1e-6
NEG_INF = -1e30
BIG_CHUNK = 1 << 30

kernel_name = 'hybrid_diff_stickbreak_hmoe'


def rms_norm(x, g):
    xf = x.astype(jnp.float32)
    y = xf * lax.rsqrt(jnp.mean(xf * xf, axis=-1, keepdims=True) + EPS)
    return (y * g.astype(jnp.float32)).astype(x.dtype)


def chunk_ids(pos, length):
    real = (pos - N_META) // CHUNK + 1
    cid = jnp.where(pos < N_META, 0, real)
    return jnp.where(pos < length, cid, BIG_CHUNK)


def t5_bucket(rel):
    half = N_BUCKETS // 2
    max_exact = half // 2
    ret = jnp.where(rel > 0, half, 0)
    n = jnp.abs(rel)
    nf = jnp.maximum(n, 1).astype(jnp.float32)
    large = max_exact + (jnp.log(nf / max_exact) / math.log(MAX_DIST / max_exact)
                         * (half - max_exact)).astype(jnp.int32)
    large = jnp.minimum(large, half - 1)
    return ret + jnp.where(n < max_exact, n, large)


def to_blocks(t):
    b, h, lp, d = t.shape
    return t.reshape(b, h, lp // QBLK, QBLK, d).transpose(2, 0, 1, 3, 4)


def from_blocks(t):
    n, b, h, q, d = t.shape
    return t.transpose(1, 2, 0, 3, 4).reshape(b, h, n * q, d)


def diff_attention(q1, q2, k1, k2, v, lam, rel_table, length):
    lp = k1.shape[2]
    n_blk = lp // QBLK
    kpos = jnp.arange(lp)
    kchunk = chunk_ids(kpos, length)
    kvalid = kpos < length
    scale = HEAD_DIM ** -0.5
    table = rel_table.astype(jnp.float32)
    vf = v.astype(jnp.float32)

    def block(args):
        b, qb1, qb2 = args
        qpos = b * QBLK + jnp.arange(QBLK)
        mask = (kchunk[None, :] <= chunk_ids(qpos, length)[:, None]) & kvalid[None, :]
        bias = table[t5_bucket(kpos[None, :] - qpos[:, None])].transpose(2, 0, 1)[None]

        def probs(qb, k):
            s = jnp.einsum('bhqd,bhkd->bhqk', qb, k, preferred_element_type=jnp.float32) * scale + bias
            return jax.nn.softmax(jnp.where(mask, s, NEG_INF), axis=-1)

        w = probs(qb1, k1) - lam * probs(qb2, k2)
        return jnp.einsum('bhqk,bhkd->bhqd', w, vf)

    out = lax.map(block, (jnp.arange(n_blk), to_blocks(q1), to_blocks(q2)))
    return from_blocks(out)


def stick_breaking(q, k, v):
    lp = k.shape[2]
    n_blk = lp // QBLK
    kpos = jnp.arange(lp)
    scale = HEAD_DIM ** -0.5
    vf = v.astype(jnp.float32)

    def block(args):
        b, qb = args
        qpos = b * QBLK + jnp.arange(QBLK)
        mask = (kpos[None, :] < qpos[:, None])[None, None]
        z = jnp.einsum('bhqd,bhkd->bhqk', qb, k, preferred_element_type=jnp.float32) * scale
        log_keep = jnp.where(mask, jax.nn.log_sigmoid(-z), 0.0)
        log_a = jax.nn.log_sigmoid(z) + lax.cumsum(log_keep, axis=3, reverse=True) - log_keep
        a = jnp.where(mask, jnp.exp(log_a), 0.0)
        return jnp.einsum('bhqk,bhkd->bhqd', a, vf)

    out = lax.map(block, (jnp.arange(n_blk), to_blocks(q)))
    return from_blocks(out)


def hier_moe(x, w_group, w_router, w_gate, w_up, w_down):
    b, l, d = x.shape
    xf = x.reshape(-1, d)
    n = xf.shape[0]
    g_prob = jax.nn.softmax(jnp.matmul(xf, w_group).astype(jnp.float32), axis=-1)
    g_idx = jnp.argmax(g_prob, axis=-1)
    g_w = jnp.take_along_axis(g_prob, g_idx[:, None], axis=1)[:, 0]
    e_logit = jnp.matmul(xf, w_router).astype(jnp.float32).reshape(n, N_GROUPS, EXPERTS_PER_GROUP)
    e_logit = jnp.take_along_axis(e_logit, g_idx[:, None, None], axis=1)[:, 0]
    e_prob = jax.nn.softmax(e_logit, axis=-1)
    top_w, top_i = lax.top_k(e_prob, EXPERT_TOP_K)
    top_w = top_w / jnp.sum(top_w, axis=-1, keepdims=True)
    eid = g_idx[:, None] * EXPERTS_PER_GROUP + top_i
    comb = jnp.sum(jax.nn.one_hot(eid, N_EXPERTS, dtype=jnp.float32)
                   * (g_w[:, None] * top_w)[..., None], axis=1)
    y = jnp.zeros((n, d), jnp.float32)
    for e in range(N_EXPERTS):
        hdn = jax.nn.silu(jnp.matmul(xf, w_gate[e])) * jnp.matmul(xf, w_up[e])
        y = y + comb[:, e:e + 1] * jnp.matmul(hdn, w_down[e]).astype(jnp.float32)
    return y.astype(x.dtype).reshape(b, l, d)


def setup_inputs(seed: int = 0) -> dict:
    key = jax.random.key(seed)
    ks = jax.random.split(key, 20)
    f32 = jnp.float32
    nrm = lambda k, shape, s: jax.random.normal(k, shape, f32) * s
    return {
        'x': nrm(ks[0], (BATCH, SEQ, D_MODEL), 1.0),
        'meta_tokens': nrm(ks[1], (N_META, D_MODEL), 1.0),
        'rel_table': nrm(ks[2], (N_BUCKETS, N_DIFF_HEADS), 0.5),
        'attn_norm': 1.0 + nrm(ks[3], (DEPTH, D_MODEL), 0.02),
        'w_in': nrm(ks[4], (DEPTH, D_MODEL, PROJ_WIDTH), D_MODEL ** -0.5),
        'lambda_q1': nrm(ks[5], (DEPTH, HEAD_DIM), 0.1),
        'lambda_k1': nrm(ks[6], (DEPTH, HEAD_DIM), 0.1),
        'lambda_q2': nrm(ks[7], (DEPTH, HEAD_DIM), 0.1),
        'lambda_k2': nrm(ks[8], (DEPTH, HEAD_DIM), 0.1),
        'diff_norm': 1.0 + nrm(ks[9], (DEPTH, DIFF_V_DIM), 0.02),
        'sb_norm': 1.0 + nrm(ks[10], (DEPTH, HEAD_DIM), 0.02),
        'w_out': nrm(ks[11], (DEPTH, MIX_WIDTH, D_MODEL), MIX_WIDTH ** -0.5),
        'ffn_norm': 1.0 + nrm(ks[12], (DEPTH, D_MODEL), 0.02),
        'w_group': nrm(ks[13], (DEPTH, D_MODEL, N_GROUPS), D_MODEL ** -0.5),
        'w_router': nrm(ks[14], (DEPTH, D_MODEL, N_EXPERTS), D_MODEL ** -0.5),
        'w_gate': nrm(ks[15], (DEPTH, N_EXPERTS, D_MODEL, D_EXPERT), D_MODEL ** -0.5),
        'w_up': nrm(ks[16], (DEPTH, N_EXPERTS, D_MODEL, D_EXPERT), D_MODEL ** -0.5),
        'w_down': nrm(ks[17], (DEPTH, N_EXPERTS, D_EXPERT, D_MODEL), D_EXPERT ** -0.5),
        'final_norm': 1.0 + nrm(ks[18], (D_MODEL,), 0.02),
    }


def reference(x, meta_tokens, rel_table, attn_norm, w_in, lambda_q1, lambda_k1, lambda_q2, lambda_k2,
              diff_norm, sb_norm, w_out, ffn_norm, w_group, w_router, w_gate, w_up, w_down, final_norm):
    b = x.shape[0]
    meta = jnp.broadcast_to(meta_tokens.astype(x.dtype)[None], (b, N_META, D_MODEL))
    h = jnp.concatenate([meta, x], axis=1)
    length = h.shape[1]
    lp = ((length + QBLK - 1) // QBLK) * QBLK
    offsets = [0]
    for s in PROJ_SIZES:
        offsets.append(offsets[-1] + s)

    for l in range(DEPTH):
        hn = rms_norm(h, attn_norm[l])
        proj = jnp.matmul(hn, w_in[l])
        proj = jnp.pad(proj, ((0, 0), (0, lp - length), (0, 0)))
        dq, dk, dv, sq, sk, sv = [proj[..., offsets[i]:offsets[i + 1]] for i in range(len(PROJ_SIZES))]
        dq = dq.reshape(b, lp, N_DIFF_HEADS, 2, HEAD_DIM).transpose(0, 2, 3, 1, 4)
        dk = dk.reshape(b, lp, N_DIFF_HEADS, 2, HEAD_DIM).transpose(0, 2, 3, 1, 4)
        dv = dv.reshape(b, lp, N_DIFF_HEADS, DIFF_V_DIM).transpose(0, 2, 1, 3)
        lambda_init = 0.8 - 0.6 * math.exp(-0.3 * l)
        lam = (jnp.exp(jnp.sum(lambda_q1[l].astype(jnp.float32) * lambda_k1[l].astype(jnp.float32)))
               - jnp.exp(jnp.sum(lambda_q2[l].astype(jnp.float32) * lambda_k2[l].astype(jnp.float32)))
               + lambda_init)
        a_out = diff_attention(dq[:, :, 0], dq[:, :, 1], dk[:, :, 0], dk[:, :, 1], dv, lam, rel_table, length)
        a_out = rms_norm(a_out, diff_norm[l]) * (1.0 - lambda_init)
        heads = lambda t: t.reshape(b, lp, N_SB_HEADS, HEAD_DIM).transpose(0, 2, 1, 3)
        s_out = rms_norm(stick_breaking(heads(sq), heads(sk), heads(sv)), sb_norm[l])
        mixed = jnp.concatenate([a_out.transpose(0, 2, 1, 3).reshape(b, lp, DIFF_WIDTH),
                                 s_out.transpose(0, 2, 1, 3).reshape(b, lp, SB_WIDTH)], axis=-1)
        mixed = mixed[:, :length].astype(h.dtype)
        h = h + jnp.matmul(mixed, w_out[l])
        h = h + hier_moe(rms_norm(h, ffn_norm[l]), w_group[l], w_router[l], w_gate[l], w_up[l], w_down[l])

    return rms_norm(h, final_norm)[:, N_META:]
```

```python
import functools
import math

import jax
import jax.numpy as jnp
from jax import lax
from jax.experimental import pallas as pl
from jax.experimental.pallas import tpu as pltpu

F32 = jnp.float32
BF16 = jnp.bfloat16
I32 = jnp.int32

D_MODEL = 1024
CHUNK = 64
N_META = 16
HEAD_DIM = 64
N_DIFF_HEADS = 4
DIFF_V_DIM = 128
N_SB_HEADS = 8
N_BUCKETS = 32
MAX_DIST = 128
N_GROUPS = 4
EXPERTS_PER_GROUP = 4
N_EXPERTS = 16
D_EXPERT = 512
EPS = 1e-6
NEG_INF = -1e30
LAMBDA_INIT = 0.8 - 0.6 * math.exp(-0.3 * 0)

PAIR_LO = (0, 0, 0, 1, 1, 2)
PAIR_HI = (1, 2, 3, 2, 3, 3)
N_CLASSES = N_GROUPS * len(PAIR_LO)

TQ = 256
T_PROJ = 512
T_MOE = 128
VMEM_LIMIT = 48 * 1024 * 1024


def _rms(x, g):
    return x * lax.rsqrt(jnp.mean(x * x, axis=-1, keepdims=True) + EPS) * g


def _proj_kernel(x_ref, g_ref, wk_ref, wqvt_ref, k_ref, qvt_ref, *, n_sub, t_sub):
    hn = _rms(x_ref[0], g_ref[...]).astype(BF16)
    k_ref[0] = jnp.dot(hn, wk_ref[...], preferred_element_type=F32).astype(BF16)
    n_rows = wqvt_ref.shape[0]
    for r in range(0, n_rows, 512):
        blk = lax.dot_general(wqvt_ref[r:r + 512, :], hn, (((1,), (1,)), ((), ())),
                              preferred_element_type=F32)
        for j in range(n_sub):
            qvt_ref[0, j, r:r + 512, :] = blk[:, j * t_sub:(j + 1) * t_sub].astype(BF16)


def _proj_call(x, g, wk, wqvt, t_tok, t_sub):
    b, n, d = x.shape
    n_sub = t_tok // t_sub
    kern = functools.partial(_proj_kernel, n_sub=n_sub, t_sub=t_sub)
    return pl.pallas_call(
        kern,
        grid=(b, n // t_tok),
        in_specs=[
            pl.BlockSpec((1, t_tok, d), lambda bi, i: (bi, i, 0)),
            pl.BlockSpec((1, d), lambda bi, i: (0, 0)),
            pl.BlockSpec(wk.shape, lambda bi, i: (0, 0)),
            pl.BlockSpec(wqvt.shape, lambda bi, i: (0, 0)),
        ],
        out_specs=[
            pl.BlockSpec((1, t_tok, wk.shape[1]), lambda bi, i: (bi, i, 0)),
            pl.BlockSpec((1, n_sub, wqvt.shape[0], t_sub), lambda bi, i: (bi, i, 0, 0)),
        ],
        out_shape=[
            jax.ShapeDtypeStruct((b, n, wk.shape[1]), BF16),
            jax.ShapeDtypeStruct((b, n // t_sub, wqvt.shape[0], t_sub), BF16),
        ],
        compiler_params=pltpu.CompilerParams(
            dimension_semantics=("arbitrary", "arbitrary"), vmem_limit_bytes=VMEM_LIMIT),
        name="proj",
    )(x, g, wk, wqvt)


def _diff_kernel(lam_ref, q_ref, k_ref, v_ref, km_ref, vm_ref, bias_ref, bm_ref, dn_ref, o_ref,
                 m1, l1, a1, m2, l2, a2):
    qi = pl.program_id(2)
    q = q_ref[0, 0]
    zq = jnp.zeros((HEAD_DIM, TQ), BF16)
    qa = jnp.concatenate([q[:HEAD_DIM], zq], axis=0)
    qb = jnp.concatenate([zq, q[HEAD_DIM:]], axis=0)

    for m_ref, l_ref, a_ref in ((m1, l1, a1), (m2, l2, a2)):
        m_ref[...] = jnp.full(m_ref.shape, NEG_INF, F32)
        l_ref[...] = jnp.zeros(l_ref.shape, F32)
        a_ref[...] = jnp.zeros(a_ref.shape, F32)

    def step(kblk, vt, bias):
        for qq, m_ref, l_ref, a_ref in ((qa, m1, l1, a1), (qb, m2, l2, a2)):
            s = jnp.dot(kblk, qq, preferred_element_type=F32)
            if bias is not None:
                s = s + bias
            m_old = m_ref[...]
            m_new = jnp.maximum(m_old, jnp.max(s, axis=0, keepdims=True))
            alpha = jnp.exp(m_old - m_new)
            p = jnp.exp(s - m_new)
            l_ref[...] = alpha * l_ref[...] + jnp.sum(p, axis=0, keepdims=True)
            a_ref[...] = alpha * a_ref[...] + jnp.dot(vt, p.astype(BF16), preferred_element_type=F32)
            m_ref[...] = m_new

    step(k_ref[0, pl.ds(pl.multiple_of(qi * TQ, TQ), TQ), :], v_ref[0, qi], bias_ref[0, 1])

    @pl.when(qi > 0)
    def _():
        kt = qi - 1
        step(k_ref[0, pl.ds(pl.multiple_of(kt * TQ, TQ), TQ), :], v_ref[0, kt], bias_ref[0, 0])

    step(km_ref[...], vm_ref[...], bm_ref[0, 0])

    def far(kt, c):
        step(k_ref[0, pl.ds(pl.multiple_of(kt * TQ, TQ), TQ), :], v_ref[0, kt], None)
        return c

    lax.fori_loop(0, qi - 1, far, 0)

    lp = lam_ref[...]
    lam = (jnp.exp(jnp.sum(lp[0:1] * lp[1:2], axis=-1, keepdims=True))
           - jnp.exp(jnp.sum(lp[2:3] * lp[3:4], axis=-1, keepdims=True)) + LAMBDA_INIT)
    o = a1[...] / l1[...] - lam * (a2[...] / l2[...])
    y = o * lax.rsqrt(jnp.mean(o * o, axis=0, keepdims=True) + EPS) * dn_ref[...]
    o_ref[0] = (y * (1.0 - LAMBDA_INIT)).T.astype(BF16)


def _diff_call(lamp, kk, qvt, k_meta, vt_meta, bias_near, bias_meta, dn_col):
    b, n, _ = kk.shape
    nt = n // TQ
    return pl.pallas_call(
        _diff_kernel,
        grid=(b, N_DIFF_HEADS, nt),
        in_specs=[
            pl.BlockSpec(lamp.shape, lambda bi, h, qi: (0, 0)),
            pl.BlockSpec((1, 1, 128, TQ), lambda bi, h, qi: (bi, qi, h, 0)),
            pl.BlockSpec((1, n, 128), lambda bi, h, qi: (bi, 0, h)),
            pl.BlockSpec((1, nt, 128, TQ), lambda bi, h, qi: (bi, 0, N_DIFF_HEADS + h, 0)),
            pl.BlockSpec((N_META, 128), lambda bi, h, qi: (0, h)),
            pl.BlockSpec((128, N_META), lambda bi, h, qi: (N_DIFF_HEADS + h, 0)),
            pl.BlockSpec((1, 2, TQ, TQ), lambda bi, h, qi: (h, 0, 0, 0)),
            pl.BlockSpec((1, 1, N_META, TQ), lambda bi, h, qi: (h, jnp.minimum(qi, 1), 0, 0)),
            pl.BlockSpec((DIFF_V_DIM, 1), lambda bi, h, qi: (0, 0)),
        ],
        out_specs=pl.BlockSpec((1, TQ, 128), lambda bi, h, qi: (bi, qi, h)),
        out_shape=jax.ShapeDtypeStruct((b, n, N_DIFF_HEADS * DIFF_V_DIM), BF16),
        scratch_shapes=[
            pltpu.VMEM((1, TQ), F32), pltpu.VMEM((1, TQ), F32), pltpu.VMEM((DIFF_V_DIM, TQ), F32),
            pltpu.VMEM((1, TQ), F32), pltpu.VMEM((1, TQ), F32), pltpu.VMEM((DIFF_V_DIM, TQ), F32),
        ],
        compiler_params=pltpu.CompilerParams(
            dimension_semantics=("arbitrary", "arbitrary", "arbitrary"), vmem_limit_bytes=VMEM_LIMIT),
        name="diff_attn",
    )(lamp, qvt, kk, qvt, k_meta, vt_meta, bias_near, bias_meta, dn_col)


def _softplus(z):
    return jnp.maximum(z, 0.0) + jnp.log(1.0 + jnp.exp(-jnp.abs(z)))


def _sb_kernel(q_ref, k_ref, v_ref, km_ref, vm_ref, g_ref, o_ref, r_ref, a_ref):
    qi = pl.program_id(2)
    q = q_ref[0, 0]
    zq = jnp.zeros((HEAD_DIM, TQ), BF16)
    qs = (jnp.concatenate([q[:HEAD_DIM], zq], axis=0), jnp.concatenate([zq, q[HEAD_DIM:]], axis=0))

    r_ref[...] = jnp.zeros(r_ref.shape, F32)
    a_ref[...] = jnp.zeros(a_ref.shape, F32)

    def tri(n):
        return (lax.broadcasted_iota(I32, (n, n), 0) <= lax.broadcasted_iota(I32, (n, n), 1)).astype(BF16)

    def step(kblk, vt, causal):
        tk = kblk.shape[0]
        t = tri(tk)
        if causal:
            keep = lax.broadcasted_iota(I32, (tk, TQ), 0) < lax.broadcasted_iota(I32, (tk, TQ), 1)
        for hh in range(2):
            z = jnp.dot(kblk, qs[hh], preferred_element_type=F32)
            sp = _softplus(z)
            if causal:
                sp = jnp.where(keep, sp, 0.0)
            hi = sp.astype(BF16)
            lo = (sp - hi.astype(F32)).astype(BF16)
            suf = (jnp.dot(t, hi, preferred_element_type=F32)
                   + jnp.dot(t, lo, preferred_element_type=F32))
            r_old = r_ref[hh:hh + 1, :]
            a = jnp.exp(z - suf - r_old)
            if causal:
                a = jnp.where(keep, a, 0.0)
            sl = slice(hh * HEAD_DIM, (hh + 1) * HEAD_DIM)
            a_ref[sl, :] += jnp.dot(vt[sl, :], a.astype(BF16), preferred_element_type=F32)
            r_ref[hh:hh + 1, :] = r_old + suf[0:1, :]

    step(k_ref[0, pl.ds(pl.multiple_of(qi * TQ, TQ), TQ), :], v_ref[0, qi], True)

    def left(i, c):
        kt = qi - 1 - i
        step(k_ref[0, pl.ds(pl.multiple_of(kt * TQ, TQ), TQ), :], v_ref[0, kt], False)
        return c

    lax.fori_loop(0, qi, left, 0)
    step(km_ref[...], vm_ref[...], False)

    ys = []
    for hh in range(2):
        o = a_ref[hh * HEAD_DIM:(hh + 1) * HEAD_DIM, :]
        ys.append(o * lax.rsqrt(jnp.mean(o * o, axis=0, keepdims=True) + EPS) * g_ref[...])
    o_ref[0] = jnp.concatenate(ys, axis=0).T.astype(BF16)


def _sb_call(kk, qvt, k_meta, vt_meta, g_col):
    b, n, _ = kk.shape
    nt = n // TQ
    n_pairs = N_SB_HEADS // 2
    q_off = 2 * N_DIFF_HEADS
    v_off = 3 * N_DIFF_HEADS
    return pl.pallas_call(
        _sb_kernel,
        grid=(b, n_pairs, nt),
        in_specs=[
            pl.BlockSpec((1, 1, 128, TQ), lambda bi, h, qi: (bi, qi, q_off + h, 0)),
            pl.BlockSpec((1, n, 128), lambda bi, h, qi: (bi, 0, N_DIFF_HEADS + h)),
            pl.BlockSpec((1, nt, 128, TQ), lambda bi, h, qi: (bi, 0, v_off + h, 0)),
            pl.BlockSpec((N_META, 128), lambda bi, h, qi: (0, N_DIFF_HEADS + h)),
            pl.BlockSpec((128, N_META), lambda bi, h, qi: (v_off + h, 0)),
            pl.BlockSpec((HEAD_DIM, 1), lambda bi, h, qi: (0, 0)),
        ],
        out_specs=pl.BlockSpec((1, TQ, 128), lambda bi, h, qi: (bi, qi, h)),
        out_shape=jax.ShapeDtypeStruct((b, n, N_SB_HEADS * HEAD_DIM), BF16),
        scratch_shapes=[pltpu.VMEM((2, TQ), F32), pltpu.VMEM((128, TQ), F32)],
        compiler_params=pltpu.CompilerParams(
            dimension_semantics=("arbitrary", "arbitrary", "arbitrary"), vmem_limit_bytes=VMEM_LIMIT),
        name="sb_attn",
    )(qvt, kk, qvt, k_meta, vt_meta, g_col)


def _outproj_kernel(x_ref, ma_ref, ms_ref, wa_ref, ws_ref, fg_ref, wr_ref, h_ref, rw_ref, rc_ref):
    h1 = (x_ref[0] + jnp.dot(ma_ref[0], wa_ref[...], preferred_element_type=F32)
          + jnp.dot(ms_ref[0], ws_ref[...], preferred_element_type=F32))
    h_ref[0] = h1
    hn = _rms(h1, fg_ref[...])
    lg = lax.dot_general(wr_ref[...], hn, (((1,), (1,)), ((), ())),
                         precision=lax.Precision.HIGHEST, preferred_element_type=F32)
    t = lg.shape[1]
    rows = [lg[i:i + 1, :] for i in range(N_GROUPS + N_EXPERTS)]

    def first_argmax(vals, skip=None):
        best, idx = None, None
        for j, v in enumerate(vals):
            if skip is not None:
                v = jnp.where(skip == j, -jnp.inf, v)
            if best is None:
                best, idx = v, jnp.zeros((1, t), I32)
            else:
                better = v > best
                idx = jnp.where(better, j, idx)
                best = jnp.where(better, v, best)
        return best, idx

    g_best, g_idx = first_argmax(rows[:N_GROUPS])
    g_w = 1.0 / sum(jnp.exp(r - g_best) for r in rows[:N_GROUPS])
    el = []
    for j in range(EXPERTS_PER_GROUP):
        v = rows[N_GROUPS + j]
        for g in range(1, N_GROUPS):
            v = jnp.where(g_idx == g, rows[N_GROUPS + EXPERTS_PER_GROUP * g + j], v)
        el.append(v)
    e1, i1 = first_argmax(el)
    e2, i2 = first_argmax(el, skip=i1)
    p2 = jnp.exp(e2 - e1)
    w1 = g_w / (1.0 + p2)
    w2 = g_w * p2 / (1.0 + p2)
    lo = jnp.minimum(i1, i2)
    hi = jnp.maximum(i1, i2)
    pair = jnp.where(lo == 0, 0, jnp.where(lo == 1, 3, 5)) + (hi - lo - 1)
    cls = g_idx * len(PAIR_LO) + pair
    first_is_lo = i1 < i2
    w_lo = jnp.where(first_is_lo, w1, w2)
    w_hi = jnp.where(first_is_lo, w2, w1)
    rw_ref[0, 0] = jnp.concatenate([w_lo, w_hi, jnp.zeros((6, t), F32)], axis=0)
    rc_ref[0, 0] = jnp.concatenate([cls, jnp.zeros((7, t), I32)], axis=0)


def _outproj_call(x, ma, ms, wa, ws, fg, wr, t_tok):
    b, n, d = x.shape
    nt = n // t_tok
    return pl.pallas_call(
        _outproj_kernel,
        grid=(b, nt),
        in_specs=[
            pl.BlockSpec((1, t_tok, d), lambda bi, i: (bi, i, 0)),
            pl.BlockSpec((1, t_tok, ma.shape[2]), lambda bi, i: (bi, i, 0)),
            pl.BlockSpec((1, t_tok, ms.shape[2]), lambda bi, i: (bi, i, 0)),
            pl.BlockSpec(wa.shape, lambda bi, i: (0, 0)),
            pl.BlockSpec(ws.shape, lambda bi, i: (0, 0)),
            pl.BlockSpec((1, d), lambda bi, i: (0, 0)),
            pl.BlockSpec(wr.shape, lambda bi, i: (0, 0)),
        ],
        out_specs=[
            pl.BlockSpec((1, t_tok, d), lambda bi, i: (bi, i, 0)),
            pl.BlockSpec((1, 1, 8, t_tok), lambda bi, i: (bi, i, 0, 0)),
            pl.BlockSpec((1, 1, 8, t_tok), lambda bi, i: (bi, i, 0, 0)),
        ],
        out_shape=[
            jax.ShapeDtypeStruct((b, n, d), F32),
            jax.ShapeDtypeStruct((b, nt, 8, t_tok), F32),
            jax.ShapeDtypeStruct((b, nt, 8, t_tok), I32),
        ],
        compiler_params=pltpu.CompilerParams(
            dimension_semantics=("arbitrary", "arbitrary"), vmem_limit_bytes=VMEM_LIMIT),
        name="outproj_router",
    )(x, ma, ms, wa, ws, fg, wr)


def _moe_kernel(rows_ref, nval_ref, e1_ref, e2_ref, nused_ref,
                h_hbm, ws_ref, wgu1_ref, wd1_ref, wgu2_ref, wd2_ref, fg_ref, og_ref, out_hbm,
                xbuf, obuf, gsem, ssem):
    t = pl.program_id(0)
    n_used = nused_ref[0]
    slot = lax.rem(t, 2)

    def row_copy_in(tile, r, s):
        tok = rows_ref[tile * T_MOE + r]
        return pltpu.make_async_copy(h_hbm.at[pl.ds(tok, 1), :], xbuf.at[s, pl.ds(r, 1), :], gsem.at[s])

    def row_copy_out(tile, r, s):
        tok = rows_ref[tile * T_MOE + r]
        return pltpu.make_async_copy(obuf.at[s, pl.ds(r, 1), :], out_hbm.at[pl.ds(tok, 1), :], ssem.at[s])

    def for_rows(tile, fn):
        def body(r, c):
            fn(r)
            return c
        lax.fori_loop(0, nval_ref[tile], body, 0)

    @pl.when(t == 0)
    def _():
        xbuf[...] = jnp.zeros(xbuf.shape, F32)
        for_rows(0, lambda r: row_copy_in(0, r, 0).start())

    @pl.when(t + 1 < n_used)
    def _():
        for_rows(t + 1, lambda r: row_copy_in(t + 1, r, 1 - slot).start())

    @pl.when(t < n_used)
    def _():
        for_rows(t, lambda r: row_copy_in(t, r, slot).wait())

        @pl.when(t >= 2)
        def _():
            for_rows(t - 2, lambda r: row_copy_out(t - 2, r, slot).wait())

        x = xbuf[slot]
        hn = _rms(x, fg_ref[...]).astype(BF16)
        w = ws_ref[...]
        y = jnp.zeros(x.shape, F32)
        for j, (wgu_ref, wd_ref) in enumerate(((wgu1_ref, wd1_ref), (wgu2_ref, wd2_ref))):
            gu = jnp.dot(hn, wgu_ref[0], preferred_element_type=F32)
            g = gu[:, :D_EXPERT]
            hd = g / (1.0 + jnp.exp(-g)) * gu[:, D_EXPERT:] * w[:, j:j + 1]
            y = y + jnp.dot(hd.astype(BF16), wd_ref[0], preferred_element_type=F32)
        obuf[slot] = _rms(x + y, og_ref[...])
        for_rows(t, lambda r: row_copy_out(t, r, slot).start())

        @pl.when(t == n_used - 1)
        def _():
            @pl.when(t >= 1)
            def _():
                for_rows(t - 1, lambda r: row_copy_out(t - 1, r, 1 - slot).wait())
            for_rows(t, lambda r: row_copy_out(t, r, slot).wait())


def _moe_call(rows, nval, e1, e2, nused, h1, w_slot, wgu, wd, fg, og):
    n_tok, d = h1.shape
    n_tiles = rows.shape[0] // T_MOE
    grid_spec = pltpu.PrefetchScalarGridSpec(
        num_scalar_prefetch=5,
        grid=(n_tiles,),
        in_specs=[
            pl.BlockSpec(memory_space=pl.ANY),
            pl.BlockSpec((T_MOE, 2), lambda t, *_: (t, 0)),
            pl.BlockSpec((1, d, 2 * D_EXPERT), lambda t, rows, nval, e1, e2, nu: (e1[t], 0, 0)),
            pl.BlockSpec((1, D_EXPERT, d), lambda t, rows, nval, e1, e2, nu: (e1[t], 0, 0)),
            pl.BlockSpec((1, d, 2 * D_EXPERT), lambda t, rows, nval, e1, e2, nu: (e2[t], 0, 0)),
            pl.BlockSpec((1, D_EXPERT, d), lambda t, rows, nval, e1, e2, nu: (e2[t], 0, 0)),
            pl.BlockSpec((1, d), lambda t, *_: (0, 0)),
            pl.BlockSpec((1, d), lambda t, *_: (0, 0)),
        ],
        out_specs=pl.BlockSpec(memory_space=pl.ANY),
        scratch_shapes=[
            pltpu.VMEM((2, T_MOE, d), F32),
            pltpu.VMEM((2, T_MOE, d), F32),
            pltpu.SemaphoreType.DMA((2,)),
            pltpu.SemaphoreType.DMA((2,)),
        ],
    )
    return pl.pallas_call(
        _moe_kernel,
        grid_spec=grid_spec,
        out_shape=jax.ShapeDtypeStruct((n_tok, d), F32),
        compiler_params=pltpu.CompilerParams(
            dimension_semantics=("arbitrary",), vmem_limit_bytes=VMEM_LIMIT),
        name="moe",
    )(rows, nval, e1, e2, nused, h1, w_slot, wgu, wd, wgu, wd, fg, og)


def _rel_bucket(rel):
    half = N_BUCKETS // 2
    max_exact = half // 2
    ret = jnp.where(rel > 0, half, 0)
    n = jnp.abs(rel)
    nf = jnp.maximum(n, 1).astype(F32)
    large = max_exact + (jnp.log(nf / max_exact) / math.log(MAX_DIST / max_exact)
                         * (half - max_exact)).astype(I32)
    large = jnp.minimum(large, half - 1)
    return ret + jnp.where(n < max_exact, n, large)


def _bias_tables(rel_table):
    table = rel_table.astype(F32)
    far = table[_rel_bucket(jnp.array(-8 * MAX_DIST))]
    kk = jnp.arange(TQ)[:, None]
    qq = jnp.arange(TQ)[None, :]
    prev = table[_rel_bucket(kk - TQ - qq)] - far
    diag = table[_rel_bucket(kk - qq)] - far
    diag = jnp.where((kk // CHUNK <= qq // CHUNK)[..., None], diag, NEG_INF)
    near = jnp.stack([prev, diag], axis=0).transpose(3, 0, 1, 2)
    mm = jnp.arange(N_META)[:, None]
    meta0 = table[_rel_bucket(mm - N_META - qq)] - far
    meta = jnp.stack([meta0, jnp.zeros_like(meta0)], axis=0).transpose(3, 0, 1, 2)
    return near, meta


def _route_tables(cls, w_tok, n_tiles):
    n = cls.shape[0]
    order = jnp.argsort(cls, stable=True).astype(I32)
    counts = jnp.sum(cls[:, None] == jnp.arange(N_CLASSES)[None, :], axis=0).astype(I32)
    padded = (counts + T_MOE - 1) // T_MOE * T_MOE
    pend = jnp.cumsum(padded)
    pstart = pend - padded
    ustart = jnp.cumsum(counts) - counts
    scls = cls[order]
    slot = pstart[scls] + jnp.arange(n, dtype=I32) - ustart[scls]
    rows = jnp.zeros((n_tiles * T_MOE,), I32).at[slot].set(order)
    w_slot = jnp.zeros((n_tiles * T_MOE, 2), F32).at[slot].set(w_tok[order])
    n_used = pend[-1] // T_MOE
    tile = jnp.arange(n_tiles, dtype=I32)
    tcls = jnp.searchsorted(pend, tile * T_MOE, side="right").astype(I32)
    tcls = jnp.minimum(tcls, N_CLASSES - 1)
    last = tcls[jnp.maximum(n_used - 1, 0)]
    used = tile < n_used
    tcls = jnp.where(used, tcls, last)
    nval = jnp.where(used, jnp.clip(pstart[tcls] + counts[tcls] - tile * T_MOE, 0, T_MOE), 0).astype(I32)
    grp = tcls // len(PAIR_LO)
    pair = tcls % len(PAIR_LO)
    e1 = grp * EXPERTS_PER_GROUP + jnp.array(PAIR_LO, I32)[pair]
    e2 = grp * EXPERTS_PER_GROUP + jnp.array(PAIR_HI, I32)[pair]
    return rows, w_slot, nval, e1.astype(I32), e2.astype(I32), n_used.astype(I32).reshape(1)


def kernel(x, meta_tokens, rel_table, attn_norm, w_in, lambda_q1, lambda_k1, lambda_q2, lambda_k2,
           diff_norm, sb_norm, w_out, ffn_norm, w_group, w_router, w_gate, w_up, w_down, final_norm):
    b, n, d = x.shape
    assert d == D_MODEL and attn_norm.shape[0] == 1
    t_tok = min(T_PROJ, n)
    assert n % t_tok == 0 and n % TQ == 0 and (b * n) % T_MOE == 0

    w = w_in[0]
    scale = HEAD_DIM ** -0.5
    dq, dk, dv, sq, sk, sv = [w[:, i * 512:(i + 1) * 512] for i in range(6)]
    wk = jnp.concatenate([dk, sk], axis=1).astype(BF16)
    wqvt = jnp.concatenate([dq * scale, dv, sq * scale, sv], axis=1).T.astype(BF16)
    g_attn = attn_norm[0].reshape(1, d)

    kk, qvt = _proj_call(x, g_attn, wk, wqvt, t_tok, TQ)
    meta_pad = jnp.zeros((1, 128, d), x.dtype).at[0, :N_META].set(meta_tokens.astype(x.dtype))
    kk_m, qvt_m = _proj_call(meta_pad, g_attn, wk, wqvt, 128, 128)
    k_meta = kk_m[0, :N_META]
    vt_meta = qvt_m[0, 0, :, :N_META]

    bias_near, bias_meta = _bias_tables(rel_table)
    lamp = jnp.stack([lambda_q1[0], lambda_k1[0], lambda_q2[0], lambda_k2[0]]).astype(F32)
    mixed_a = _diff_call(lamp, kk, qvt, k_meta, vt_meta, bias_near, bias_meta,
                         diff_norm[0].astype(F32).reshape(DIFF_V_DIM, 1))
    mixed_s = _sb_call(kk, qvt, k_meta, vt_meta, sb_norm[0].astype(F32).reshape(HEAD_DIM, 1))

    wo = w_out[0].astype(BF16)
    wr = jnp.concatenate([w_group[0], w_router[0], jnp.zeros((d, 12), F32)], axis=1).T
    h1, rw, rc = _outproj_call(x, mixed_a, mixed_s, wo[:512], wo[512:], ffn_norm[0].reshape(1, d),
                               wr, t_tok)

    n_tok = b * n
    cls = rc[:, :, 0, :].reshape(n_tok)
    w_tok = jnp.stack([rw[:, :, 0, :].reshape(n_tok), rw[:, :, 1, :].reshape(n_tok)], axis=1)
    n_tiles = n_tok // T_MOE + N_CLASSES
    rows, w_slot, nval, e1, e2, n_used = _route_tables(cls, w_tok, n_tiles)

    wgu = jnp.concatenate([w_gate[0], w_up[0]], axis=2).astype(BF16)
    wd = w_down[0].astype(BF16)
    out = _moe_call(rows, nval, e1, e2, n_used, h1.reshape(n_tok, d), w_slot, wgu, wd,
                    ffn_norm[0].reshape(1, d), final_norm.reshape(1, d))
    return out.reshape(b, n, d)
```

```python
import functools
import math

import jax
import jax.numpy as jnp
from jax import lax
from jax.experimental import pallas as pl
from jax.experimental.pallas import tpu as pltpu

F32 = jnp.float32
BF16 = jnp.bfloat16
I32 = jnp.int32

D_MODEL = 1024
CHUNK = 64
N_META = 16
HEAD_DIM = 64
N_DIFF_HEADS = 4
DIFF_V_DIM = 128
N_SB_HEADS = 8
N_BUCKETS = 32
MAX_DIST = 128
N_GROUPS = 4
EXPERTS_PER_GROUP = 4
N_EXPERTS = 16
D_EXPERT = 512
EPS = 1e-6
NEG_INF = -1e30
LAMBDA_INIT = 0.8 - 0.6 * math.exp(-0.3 * 0)

PAIR_LO = (0, 0, 0, 1, 1, 2)
PAIR_HI = (1, 2, 3, 2, 3, 3)
N_CLASSES = N_GROUPS * len(PAIR_LO)

TQ = 256
FAR_TILES = 4
SB_GROUP = 4
SB_UNDERFLOW = 110.0
T_PROJ = 512
T_MOE = 128
VMEM_LIMIT = 48 * 1024 * 1024


def _rms(x, g):
    return x * lax.rsqrt(jnp.mean(x * x, axis=-1, keepdims=True) + EPS) * g


def _proj_kernel(x_ref, g_ref, wk_ref, wqvt_ref, k_ref, qvt_ref, *, n_sub, t_sub):
    hn = _rms(x_ref[0], g_ref[...]).astype(BF16)
    k_ref[0] = jnp.dot(hn, wk_ref[...], preferred_element_type=F32).astype(BF16)
    n_rows = wqvt_ref.shape[0]
    for r in range(0, n_rows, 512):
        blk = lax.dot_general(wqvt_ref[r:r + 512, :], hn, (((1,), (1,)), ((), ())),
                              preferred_element_type=F32)
        for j in range(n_sub):
            qvt_ref[0, j, r:r + 512, :] = blk[:, j * t_sub:(j + 1) * t_sub].astype(BF16)


def _proj_call(x, g, wk, wqvt, t_tok, t_sub):
    b, n, d = x.shape
    n_sub = t_tok // t_sub
    kern = functools.partial(_proj_kernel, n_sub=n_sub, t_sub=t_sub)
    return pl.pallas_call(
        kern,
        grid=(b, n // t_tok),
        in_specs=[
            pl.BlockSpec((1, t_tok, d), lambda bi, i: (bi, i, 0)),
            pl.BlockSpec((1, d), lambda bi, i: (0, 0)),
            pl.BlockSpec(wk.shape, lambda bi, i: (0, 0)),
            pl.BlockSpec(wqvt.shape, lambda bi, i: (0, 0)),
        ],
        out_specs=[
            pl.BlockSpec((1, t_tok, wk.shape[1]), lambda bi, i: (bi, i, 0)),
            pl.BlockSpec((1, n_sub, wqvt.shape[0], t_sub), lambda bi, i: (bi, i, 0, 0)),
        ],
        out_shape=[
            jax.ShapeDtypeStruct((b, n, wk.shape[1]), BF16),
            jax.ShapeDtypeStruct((b, n // t_sub, wqvt.shape[0], t_sub), BF16),
        ],
        compiler_params=pltpu.CompilerParams(
            dimension_semantics=("arbitrary", "arbitrary"), vmem_limit_bytes=VMEM_LIMIT),
        name="proj",
    )(x, g, wk, wqvt)


def _diff_kernel(lam_ref, q_ref, k_ref, v_ref, km_ref, vm_ref, bias_ref, bm_ref, dn_ref, o_ref,
                 m1, l1, a1, m2, l2, a2):
    qi = pl.program_id(2)
    q = q_ref[0, 0]
    zq = jnp.zeros((HEAD_DIM, TQ), BF16)
    qa = jnp.concatenate([q[:HEAD_DIM], zq], axis=0)
    qb = jnp.concatenate([zq, q[HEAD_DIM:]], axis=0)

    for m_ref, l_ref, a_ref in ((m1, l1, a1), (m2, l2, a2)):
        m_ref[...] = jnp.full(m_ref.shape, NEG_INF, F32)
        l_ref[...] = jnp.zeros(l_ref.shape, F32)
        a_ref[...] = jnp.zeros(a_ref.shape, F32)

    def step(parts):
        for qq, m_ref, l_ref, a_ref in ((qa, m1, l1, a1), (qb, m2, l2, a2)):
            scores = []
            for kblk, _, bias in parts:
                s = jnp.dot(kblk, qq, preferred_element_type=F32)
                scores.append(s if bias is None else s + bias)
            m_old = m_ref[...]
            m_new = m_old
            for s in scores:
                m_new = jnp.maximum(m_new, jnp.max(s, axis=0, keepdims=True))
            alpha = jnp.exp(m_old - m_new)
            l_new = alpha * l_ref[...]
            acc = alpha * a_ref[...]
            for (_, vts, _), s in zip(parts, scores):
                p = jnp.exp(s - m_new)
                l_new = l_new + jnp.sum(p, axis=0, keepdims=True)
                pb = p.astype(BF16)
                off = 0
                for vt in vts:
                    w = vt.shape[1]
                    acc = acc + jnp.dot(vt, pb[off:off + w], preferred_element_type=F32)
                    off += w
            l_ref[...] = l_new
            a_ref[...] = acc
            m_ref[...] = m_new

    def k_rows(tile, n_tiles):
        return k_ref[0, pl.ds(pl.multiple_of(tile * TQ, TQ), n_tiles * TQ), :]

    t0 = jnp.maximum(qi - 1, 0)
    step([(km_ref[...], [vm_ref[...]], bm_ref[0, 0]),
          (k_rows(t0, 2), [v_ref[0, t0], v_ref[0, t0 + 1]], bias_ref[0, 0])])

    n_big = t0 // FAR_TILES

    def far_big(c, carry):
        t = c * FAR_TILES
        step([(k_rows(t, FAR_TILES), [v_ref[0, t + j] for j in range(FAR_TILES)], None)])
        return carry

    lax.fori_loop(0, n_big, far_big, 0)

    def far_small(t, carry):
        step([(k_rows(t, 1), [v_ref[0, t]], None)])
        return carry

    lax.fori_loop(n_big * FAR_TILES, t0, far_small, 0)

    lp = lam_ref[...]
    lam = (jnp.exp(jnp.sum(lp[0:1] * lp[1:2], axis=-1, keepdims=True))
           - jnp.exp(jnp.sum(lp[2:3] * lp[3:4], axis=-1, keepdims=True)) + LAMBDA_INIT)
    o = a1[...] / l1[...] - lam * (a2[...] / l2[...])
    y = o * lax.rsqrt(jnp.mean(o * o, axis=0, keepdims=True) + EPS) * dn_ref[...]
    o_ref[0] = (y * (1.0 - LAMBDA_INIT)).T.astype(BF16)


def _diff_call(lamp, kk, qvt, k_meta, vt_meta, bias_near, bias_meta, dn_col):
    b, n, _ = kk.shape
    nt = n // TQ
    return pl.pallas_call(
        _diff_kernel,
        grid=(b, N_DIFF_HEADS, nt),
        in_specs=[
            pl.BlockSpec(lamp.shape, lambda bi, h, qi: (0, 0)),
            pl.BlockSpec((1, 1, 128, TQ), lambda bi, h, qi: (bi, qi, h, 0)),
            pl.BlockSpec((1, n, 128), lambda bi, h, qi: (bi, 0, h)),
            pl.BlockSpec((1, nt, 128, TQ), lambda bi, h, qi: (bi, 0, N_DIFF_HEADS + h, 0)),
            pl.BlockSpec((N_META, 128), lambda bi, h, qi: (0, h)),
            pl.BlockSpec((128, N_META), lambda bi, h, qi: (N_DIFF_HEADS + h, 0)),
            pl.BlockSpec((1, 1, 2 * TQ, TQ), lambda bi, h, qi: (h, jnp.minimum(qi, 1), 0, 0)),
            pl.BlockSpec((1, 1, N_META, TQ), lambda bi, h, qi: (h, jnp.minimum(qi, 1), 0, 0)),
            pl.BlockSpec((DIFF_V_DIM, 1), lambda bi, h, qi: (0, 0)),
        ],
        out_specs=pl.BlockSpec((1, TQ, 128), lambda bi, h, qi: (bi, qi, h)),
        out_shape=jax.ShapeDtypeStruct((b, n, N_DIFF_HEADS * DIFF_V_DIM), BF16),
        scratch_shapes=[
            pltpu.VMEM((1, TQ), F32), pltpu.VMEM((1, TQ), F32), pltpu.VMEM((DIFF_V_DIM, TQ), F32),
            pltpu.VMEM((1, TQ), F32), pltpu.VMEM((1, TQ), F32), pltpu.VMEM((DIFF_V_DIM, TQ), F32),
        ],
        compiler_params=pltpu.CompilerParams(
            dimension_semantics=("arbitrary", "arbitrary", "arbitrary"), vmem_limit_bytes=VMEM_LIMIT),
        name="diff_attn",
    )(lamp, qvt, kk, qvt, k_meta, vt_meta, bias_near, bias_meta, dn_col)


def _softplus(z):
    return jnp.maximum(z, 0.0) + jnp.log(1.0 + jnp.exp(-jnp.abs(z)))


def _sb_kernel(q_ref, k_ref, v_ref, km_ref, vm_ref, g_ref, o_ref, r_ref, a_ref):
    qi = pl.program_id(2)
    q = q_ref[0, 0]
    zq = jnp.zeros((HEAD_DIM, TQ), BF16)

    def q_padded(g):
        qg = q[g * HEAD_DIM:(g + 1) * HEAD_DIM]
        return jnp.concatenate([qg, zq] if g % 2 == 0 else [zq, qg], axis=0)

    qs = [q_padded(g) for g in range(SB_GROUP)]
    heads = [(g, slice(g * HEAD_DIM, (g + 1) * HEAD_DIM), slice((g // 2) * 128, (g // 2 + 1) * 128))
             for g in range(SB_GROUP)]

    def tri(n):
        return (lax.broadcasted_iota(I32, (n, n), 0) <= lax.broadcasted_iota(I32, (n, n), 1)).astype(BF16)

    def suffix_sums(t, sp):
        hi = sp.astype(BF16)
        lo = (sp - hi.astype(F32)).astype(BF16)
        return jnp.dot(t, hi, preferred_element_type=F32) + jnp.dot(t, lo, preferred_element_type=F32)

    def k_rows(tile, n_tiles):
        return k_ref[0, pl.ds(pl.multiple_of(tile * TQ, TQ), n_tiles * TQ), :]

    t0 = jnp.maximum(qi - 1, 0)
    has_prev = qi > 0
    k_prev, k_diag = k_rows(t0, 1), k_rows(qi, 1)
    v_prev, v_diag = v_ref[0, t0], v_ref[0, qi]
    keep = lax.broadcasted_iota(I32, (TQ, TQ), 0) < lax.broadcasted_iota(I32, (TQ, TQ), 1)
    t_full = tri(TQ)
    for g, rows, lanes in heads:
        z_diag = jnp.where(keep, jnp.dot(k_diag[:, lanes], qs[g], preferred_element_type=F32), NEG_INF)
        z_prev = jnp.dot(k_prev[:, lanes], qs[g], preferred_element_type=F32)
        suf_diag = suffix_sums(t_full, _softplus(z_diag))
        suf_prev = suffix_sums(t_full, _softplus(z_prev))
        tot_diag = suf_diag[0:1, :]
        a_diag = jnp.exp(z_diag - suf_diag).astype(BF16)
        a_prev = jnp.exp(z_prev - suf_prev - tot_diag).astype(BF16)
        acc_prev = jnp.dot(v_prev[rows, :], a_prev, preferred_element_type=F32)
        a_ref[rows, :] = (jnp.dot(v_diag[rows, :], a_diag, preferred_element_type=F32)
                          + jnp.where(has_prev, acc_prev, 0.0))
        r_ref[g:g + 1, :] = tot_diag + jnp.where(has_prev, suf_prev[0:1, :], 0.0)

    def step(kblk, vt):
        t = tri(kblk.shape[0])
        for g, rows, lanes in heads:
            z = jnp.dot(kblk[:, lanes], qs[g], preferred_element_type=F32)
            suf = suffix_sums(t, _softplus(z))
            r_old = r_ref[g:g + 1, :]
            a = jnp.exp(z - suf - r_old)
            a_ref[rows, :] += jnp.dot(vt[rows, :], a.astype(BF16), preferred_element_type=F32)
            r_ref[g:g + 1, :] = r_old + suf[0:1, :]

    def more(c):
        i, r_min = c
        return jnp.logical_and(i < t0, r_min < SB_UNDERFLOW)

    def left(c):
        i, _ = c
        kt = t0 - 1 - i
        step(k_rows(kt, 1), v_ref[0, kt])
        return i + 1, jnp.min(r_ref[...])

    _, r_min = lax.while_loop(more, left, (jnp.int32(0), jnp.min(r_ref[...])))

    @pl.when(r_min < SB_UNDERFLOW)
    def _():
        step(km_ref[...], vm_ref[...])

    ys = []
    for g, rows, _ in heads:
        o = a_ref[rows, :]
        ys.append(o * lax.rsqrt(jnp.mean(o * o, axis=0, keepdims=True) + EPS) * g_ref[...])
    o_ref[0] = jnp.concatenate(ys, axis=0).T.astype(BF16)


def _sb_call(kk, qvt, k_meta, vt_meta, g_col):
    b, n, _ = kk.shape
    nt = n // TQ
    w = SB_GROUP * HEAD_DIM
    n_groups = N_SB_HEADS // SB_GROUP
    k_off = N_DIFF_HEADS * 128 // w
    q_off = 2 * N_DIFF_HEADS * 128 // w
    v_off = 3 * N_DIFF_HEADS * 128 // w
    return pl.pallas_call(
        _sb_kernel,
        grid=(b, n_groups, nt),
        in_specs=[
            pl.BlockSpec((1, 1, w, TQ), lambda bi, h, qi: (bi, qi, q_off + h, 0)),
            pl.BlockSpec((1, n, w), lambda bi, h, qi: (bi, 0, k_off + h)),
            pl.BlockSpec((1, nt, w, TQ), lambda bi, h, qi: (bi, 0, v_off + h, 0)),
            pl.BlockSpec((N_META, w), lambda bi, h, qi: (0, k_off + h)),
            pl.BlockSpec((w, N_META), lambda bi, h, qi: (v_off + h, 0)),
            pl.BlockSpec((HEAD_DIM, 1), lambda bi, h, qi: (0, 0)),
        ],
        out_specs=pl.BlockSpec((1, TQ, w), lambda bi, h, qi: (bi, qi, h)),
        out_shape=jax.ShapeDtypeStruct((b, n, N_SB_HEADS * HEAD_DIM), BF16),
        scratch_shapes=[pltpu.VMEM((SB_GROUP, TQ), F32), pltpu.VMEM((w, TQ), F32)],
        compiler_params=pltpu.CompilerParams(
            dimension_semantics=("arbitrary", "arbitrary", "arbitrary"), vmem_limit_bytes=VMEM_LIMIT),
        name="sb_attn",
    )(qvt, kk, qvt, k_meta, vt_meta, g_col)


def _outproj_kernel(x_ref, ma_ref, ms_ref, wa_ref, ws_ref, fg_ref, wr_ref, h_ref, rw_ref, rc_ref):
    h1 = (x_ref[0] + jnp.dot(ma_ref[0], wa_ref[...], preferred_element_type=F32)
          + jnp.dot(ms_ref[0], ws_ref[...], preferred_element_type=F32))
    h_ref[0] = h1
    hn = _rms(h1, fg_ref[...])
    lg = lax.dot_general(wr_ref[...], hn, (((1,), (1,)), ((), ())),
                         precision=lax.Precision.HIGHEST, preferred_element_type=F32)
    t = lg.shape[1]
    rows = [lg[i:i + 1, :] for i in range(N_GROUPS + N_EXPERTS)]

    def first_argmax(vals, skip=None):
        best, idx = None, None
        for j, v in enumerate(vals):
            if skip is not None:
                v = jnp.where(skip == j, -jnp.inf, v)
            if best is None:
                best, idx = v, jnp.zeros((1, t), I32)
            else:
                better = v > best
                idx = jnp.where(better, j, idx)
                best = jnp.where(better, v, best)
        return best, idx

    g_best, g_idx = first_argmax(rows[:N_GROUPS])
    g_w = 1.0 / sum(jnp.exp(r - g_best) for r in rows[:N_GROUPS])
    el = []
    for j in range(EXPERTS_PER_GROUP):
        v = rows[N_GROUPS + j]
        for g in range(1, N_GROUPS):
            v = jnp.where(g_idx == g, rows[N_GROUPS + EXPERTS_PER_GROUP * g + j], v)
        el.append(v)
    e1, i1 = first_argmax(el)
    e2, i2 = first_argmax(el, skip=i1)
    p2 = jnp.exp(e2 - e1)
    w1 = g_w / (1.0 + p2)
    w2 = g_w * p2 / (1.0 + p2)
    lo = jnp.minimum(i1, i2)
    hi = jnp.maximum(i1, i2)
    pair = jnp.where(lo == 0, 0, jnp.where(lo == 1, 3, 5)) + (hi - lo - 1)
    cls = g_idx * len(PAIR_LO) + pair
    first_is_lo = i1 < i2
    w_lo = jnp.where(first_is_lo, w1, w2)
    w_hi = jnp.where(first_is_lo, w2, w1)
    rw_ref[0, 0] = jnp.concatenate([w_lo, w_hi, jnp.zeros((6, t), F32)], axis=0)
    rc_ref[0, 0] = jnp.concatenate([cls, jnp.zeros((7, t), I32)], axis=0)


def _outproj_call(x, ma, ms, wa, ws, fg, wr, t_tok):
    b, n, d = x.shape
    nt = n // t_tok
    return pl.pallas_call(
        _outproj_kernel,
        grid=(b, nt),
        in_specs=[
            pl.BlockSpec((1, t_tok, d), lambda bi, i: (bi, i, 0)),
            pl.BlockSpec((1, t_tok, ma.shape[2]), lambda bi, i: (bi, i, 0)),
            pl.BlockSpec((1, t_tok, ms.shape[2]), lambda bi, i: (bi, i, 0)),
            pl.BlockSpec(wa.shape, lambda bi, i: (0, 0)),
            pl.BlockSpec(ws.shape, lambda bi, i: (0, 0)),
            pl.BlockSpec((1, d), lambda bi, i: (0, 0)),
            pl.BlockSpec(wr.shape, lambda bi, i: (0, 0)),
        ],
        out_specs=[
            pl.BlockSpec((1, t_tok, d), lambda bi, i: (bi, i, 0)),
            pl.BlockSpec((1, 1, 8, t_tok), lambda bi, i: (bi, i, 0, 0)),
            pl.BlockSpec((1, 1, 8, t_tok), lambda bi, i: (bi, i, 0, 0)),
        ],
        out_shape=[
            jax.ShapeDtypeStruct((b, n, d), F32),
            jax.ShapeDtypeStruct((b, nt, 8, t_tok), F32),
            jax.ShapeDtypeStruct((b, nt, 8, t_tok), I32),
        ],
        compiler_params=pltpu.CompilerParams(
            dimension_semantics=("arbitrary", "arbitrary"), vmem_limit_bytes=VMEM_LIMIT),
        name="outproj_router",
    )(x, ma, ms, wa, ws, fg, wr)


def _moe_kernel(rows_ref, nval_ref, e1_ref, e2_ref, nused_ref,
                h_hbm, ws_ref, wgu1_ref, wd1_ref, wgu2_ref, wd2_ref, fg_ref, og_ref, out_hbm,
                xbuf, obuf, gsem, ssem):
    t = pl.program_id(0)
    n_used = nused_ref[0]
    slot = lax.rem(t, 2)

    def row_copy_in(tile, r, s):
        tok = rows_ref[tile * T_MOE + r]
        return pltpu.make_async_copy(h_hbm.at[pl.ds(tok, 1), :], xbuf.at[s, pl.ds(r, 1), :], gsem.at[s])

    def row_copy_out(tile, r, s):
        tok = rows_ref[tile * T_MOE + r]
        return pltpu.make_async_copy(obuf.at[s, pl.ds(r, 1), :], out_hbm.at[pl.ds(tok, 1), :], ssem.at[s])

    def for_rows(tile, fn):
        def body(r, c):
            fn(r)
            return c
        lax.fori_loop(0, nval_ref[tile], body, 0)

    @pl.when(t == 0)
    def _():
        xbuf[...] = jnp.zeros(xbuf.shape, F32)
        for_rows(0, lambda r: row_copy_in(0, r, 0).start())

    @pl.when(t + 1 < n_used)
    def _():
        for_rows(t + 1, lambda r: row_copy_in(t + 1, r, 1 - slot).start())

    @pl.when(t < n_used)
    def _():
        for_rows(t, lambda r: row_copy_in(t, r, slot).wait())

        @pl.when(t >= 2)
        def _():
            for_rows(t - 2, lambda r: row_copy_out(t - 2, r, slot).wait())

        x = xbuf[slot]
        hn = _rms(x, fg_ref[...]).astype(BF16)
        w = ws_ref[...]
        y = jnp.zeros(x.shape, F32)
        for j, (wgu_ref, wd_ref) in enumerate(((wgu1_ref, wd1_ref), (wgu2_ref, wd2_ref))):
            gu = jnp.dot(hn, wgu_ref[0], preferred_element_type=F32)
            g = gu[:, :D_EXPERT]
            hd = g / (1.0 + jnp.exp(-g)) * gu[:, D_EXPERT:] * w[:, j:j + 1]
            y = y + jnp.dot(hd.astype(BF16), wd_ref[0], preferred_element_type=F32)
        obuf[slot] = _rms(x + y, og_ref[...])
        for_rows(t, lambda r: row_copy_out(t, r, slot).start())

        @pl.when(t == n_used - 1)
        def _():
            @pl.when(t >= 1)
            def _():
                for_rows(t - 1, lambda r: row_copy_out(t - 1, r, 1 - slot).wait())
            for_rows(t, lambda r: row_copy_out(t, r, slot).wait())


def _moe_call(rows, nval, e1, e2, nused, h1, w_slot, wgu, wd, fg, og):
    n_tok, d = h1.shape
    n_tiles = rows.shape[0] // T_MOE
    grid_spec = pltpu.PrefetchScalarGridSpec(
        num_scalar_prefetch=5,
        grid=(n_tiles,),
        in_specs=[
            pl.BlockSpec(memory_space=pl.ANY),
            pl.BlockSpec((T_MOE, 2), lambda t, *_: (t, 0)),
            pl.BlockSpec((1, d, 2 * D_EXPERT), lambda t, rows, nval, e1, e2, nu: (e1[t], 0, 0)),
            pl.BlockSpec((1, D_EXPERT, d), lambda t, rows, nval, e1, e2, nu: (e1[t], 0, 0)),
            pl.BlockSpec((1, d, 2 * D_EXPERT), lambda t, rows, nval, e1, e2, nu: (e2[t], 0, 0)),
            pl.BlockSpec((1, D_EXPERT, d), lambda t, rows, nval, e1, e2, nu: (e2[t], 0, 0)),
            pl.BlockSpec((1, d), lambda t, *_: (0, 0)),
            pl.BlockSpec((1, d), lambda t, *_: (0, 0)),
        ],
        out_specs=pl.BlockSpec(memory_space=pl.ANY),
        scratch_shapes=[
            pltpu.VMEM((2, T_MOE, d), F32),
            pltpu.VMEM((2, T_MOE, d), F32),
            pltpu.SemaphoreType.DMA((2,)),
            pltpu.SemaphoreType.DMA((2,)),
        ],
    )
    return pl.pallas_call(
        _moe_kernel,
        grid_spec=grid_spec,
        out_shape=jax.ShapeDtypeStruct((n_tok, d), F32),
        compiler_params=pltpu.CompilerParams(
            dimension_semantics=("arbitrary",), vmem_limit_bytes=VMEM_LIMIT),
        name="moe",
    )(rows, nval, e1, e2, nused, h1, w_slot, wgu, wd, wgu, wd, fg, og)


def _rel_bucket(rel):
    half = N_BUCKETS // 2
    max_exact = half // 2
    ret = jnp.where(rel > 0, half, 0)
    n = jnp.abs(rel)
    nf = jnp.maximum(n, 1).astype(F32)
    large = max_exact + (jnp.log(nf / max_exact) / math.log(MAX_DIST / max_exact)
                         * (half - max_exact)).astype(I32)
    large = jnp.minimum(large, half - 1)
    return ret + jnp.where(n < max_exact, n, large)


def _bias_tables(rel_table):
    table = rel_table.astype(F32).T

    def lookup(rel):
        onehot = _rel_bucket(rel)[None, ..., None] == jnp.arange(N_BUCKETS)
        return jnp.sum(jnp.where(onehot, table[:, None, None, :], 0.0), axis=-1)

    far = lookup(jnp.full((1, 1), -8 * MAX_DIST))
    kk = jnp.arange(TQ)[:, None]
    qq = jnp.arange(TQ)[None, :]
    prev = lookup(kk - TQ - qq) - far
    diag = jnp.where(kk // CHUNK <= qq // CHUNK, lookup(kk - qq) - far, NEG_INF)
    masked = jnp.full_like(diag, NEG_INF)
    near = jnp.stack([jnp.concatenate([diag, masked], axis=1),
                      jnp.concatenate([prev, diag], axis=1)], axis=1)
    mm = jnp.arange(N_META)[:, None]
    meta0 = lookup(mm - N_META - qq) - far
    meta = jnp.stack([meta0, jnp.zeros_like(meta0)], axis=1)
    return near, meta


def _route_tables(cls, w_tok, n_tiles):
    n = cls.shape[0]
    order = jnp.argsort(cls, stable=True).astype(I32)
    counts = jnp.sum(cls[:, None] == jnp.arange(N_CLASSES)[None, :], axis=0).astype(I32)
    padded = (counts + T_MOE - 1) // T_MOE * T_MOE
    pend = jnp.cumsum(padded)
    pstart = pend - padded
    ustart = jnp.cumsum(counts) - counts
    n_used = pend[-1] // T_MOE
    tile = jnp.arange(n_tiles, dtype=I32)
    tcls = jnp.sum(pend[None, :] <= (tile * T_MOE)[:, None], axis=1).astype(I32)
    tcls = jnp.minimum(tcls, N_CLASSES - 1)
    last = tcls[jnp.maximum(n_used - 1, 0)]
    used = tile < n_used
    tcls = jnp.where(used, tcls, last)
    nval = jnp.where(used, jnp.clip(pstart[tcls] + counts[tcls] - tile * T_MOE, 0, T_MOE), 0).astype(I32)
    r = jnp.arange(T_MOE, dtype=I32)
    src = (ustart[tcls] + tile * T_MOE - pstart[tcls])[:, None] + r[None, :]
    valid = r[None, :] < nval[:, None]
    rows = jnp.where(valid, order[jnp.clip(src, 0, n - 1)], 0).reshape(-1)
    w_slot = jnp.where(valid.reshape(-1, 1), w_tok[rows], 0.0)
    grp = tcls // len(PAIR_LO)
    pair = tcls % len(PAIR_LO)
    e1 = grp * EXPERTS_PER_GROUP + jnp.array(PAIR_LO, I32)[pair]
    e2 = grp * EXPERTS_PER_GROUP + jnp.array(PAIR_HI, I32)[pair]
    return rows, w_slot, nval, e1.astype(I32), e2.astype(I32), n_used.astype(I32).reshape(1)


def kernel(x, meta_tokens, rel_table, attn_norm, w_in, lambda_q1, lambda_k1, lambda_q2, lambda_k2,
           diff_norm, sb_norm, w_out, ffn_norm, w_group, w_router, w_gate, w_up, w_down, final_norm):
    b, n, d = x.shape
    assert d == D_MODEL and attn_norm.shape[0] == 1
    t_tok = min(T_PROJ, n)
    assert n % t_tok == 0 and n % TQ == 0 and (b * n) % T_MOE == 0

    w = w_in[0]
    scale = HEAD_DIM ** -0.5
    dq, dk, dv, sq, sk, sv = [w[:, i * 512:(i + 1) * 512] for i in range(6)]
    wk = jnp.concatenate([dk, sk], axis=1).astype(BF16)
    wqvt = jnp.concatenate([dq * scale, dv, sq * scale, sv], axis=1).T.astype(BF16)
    g_attn = attn_norm[0].reshape(1, d)

    kk, qvt = _proj_call(x, g_attn, wk, wqvt, t_tok, TQ)
    meta_pad = jnp.zeros((1, 128, d), x.dtype).at[0, :N_META].set(meta_tokens.astype(x.dtype))
    kk_m, qvt_m = _proj_call(meta_pad, g_attn, wk, wqvt, 128, 128)
    k_meta = kk_m[0, :N_META]
    vt_meta = qvt_m[0, 0, :, :N_META]

    bias_near, bias_meta = _bias_tables(rel_table)
    lamp = jnp.stack([lambda_q1[0], lambda_k1[0], lambda_q2[0], lambda_k2[0]]).astype(F32)
    mixed_a = _diff_call(lamp, kk, qvt, k_meta, vt_meta, bias_near, bias_meta,
                         diff_norm[0].astype(F32).reshape(DIFF_V_DIM, 1))
    mixed_s = _sb_call(kk, qvt, k_meta, vt_meta, sb_norm[0].astype(F32).reshape(HEAD_DIM, 1))

    wo = w_out[0].astype(BF16)
    wr = jnp.concatenate([w_group[0], w_router[0], jnp.zeros((d, 12), F32)], axis=1).T
    h1, rw, rc = _outproj_call(x, mixed_a, mixed_s, wo[:512], wo[512:], ffn_norm[0].reshape(1, d),
                               wr, t_tok)

    n_tok = b * n
    cls = rc[:, :, 0, :].reshape(n_tok)
    w_tok = jnp.stack([rw[:, :, 0, :].reshape(n_tok), rw[:, :, 1, :].reshape(n_tok)], axis=1)
    n_tiles = n_tok // T_MOE + N_CLASSES
    rows, w_slot, nval, e1, e2, n_used = _route_tables(cls, w_tok, n_tiles)

    wgu = jnp.concatenate([w_gate[0], w_up[0]], axis=2).astype(BF16)
    wd = w_down[0].astype(BF16)
    out = _moe_call(rows, nval, e1, e2, n_used, h1.reshape(n_tok, d), w_slot, wgu, wd,
                    ffn_norm[0].reshape(1, d), final_norm.reshape(1, d))
    return out.reshape(b, n, d)
```

```python
import functools
import math

import jax
import jax.numpy as jnp
from jax import lax
from jax.experimental import pallas as pl
from jax.experimental.pallas import tpu as pltpu

F32 = jnp.float32
BF16 = jnp.bfloat16
I32 = jnp.int32

D_MODEL = 1024
CHUNK = 64
N_META = 16
HEAD_DIM = 64
N_DIFF_HEADS = 4
DIFF_V_DIM = 128
N_SB_HEADS = 8
N_BUCKETS = 32
MAX_DIST = 128
N_GROUPS = 4
EXPERTS_PER_GROUP = 4
N_EXPERTS = 16
D_EXPERT = 512
EPS = 1e-6
NEG_INF = -1e30
LAMBDA_INIT = 0.8 - 0.6 * math.exp(-0.3 * 0)
LOG2E = math.log2(math.e)

PAIR_LO = (0, 0, 0, 1, 1, 2)
PAIR_HI = (1, 2, 3, 2, 3, 3)
N_CLASSES = N_GROUPS * len(PAIR_LO)

TQ = 256
SB_GROUP = 4
SB_UNDERFLOW = 110.0
T_PROJ = 512
T_MOE = 256
D_ROW = D_MODEL + 128
VMEM_LIMIT = 48 * 1024 * 1024


def _rms(x, g):
    return x * lax.rsqrt(jnp.mean(x * x, axis=-1, keepdims=True) + EPS) * g


def _proj_kernel(x_ref, g_ref, wk_ref, wqvt_ref, k_ref, qvt_ref, *, n_sub, t_sub):
    hn = _rms(x_ref[0], g_ref[...]).astype(BF16)
    k_ref[0] = jnp.dot(hn, wk_ref[...], preferred_element_type=F32).astype(BF16)
    n_rows = wqvt_ref.shape[0]
    for r in range(0, n_rows, 512):
        blk = lax.dot_general(wqvt_ref[r:r + 512, :], hn, (((1,), (1,)), ((), ())),
                              preferred_element_type=F32)
        for j in range(n_sub):
            qvt_ref[0, j, r:r + 512, :] = blk[:, j * t_sub:(j + 1) * t_sub].astype(BF16)


def _proj_call(x, g, wk, wqvt, t_tok, t_sub):
    b, n, d = x.shape
    n_sub = t_tok // t_sub
    kern = functools.partial(_proj_kernel, n_sub=n_sub, t_sub=t_sub)
    return pl.pallas_call(
        kern,
        grid=(b, n // t_tok),
        in_specs=[
            pl.BlockSpec((1, t_tok, d), lambda bi, i: (bi, i, 0)),
            pl.BlockSpec((1, d), lambda bi, i: (0, 0)),
            pl.BlockSpec(wk.shape, lambda bi, i: (0, 0)),
            pl.BlockSpec(wqvt.shape, lambda bi, i: (0, 0)),
        ],
        out_specs=[
            pl.BlockSpec((1, t_tok, wk.shape[1]), lambda bi, i: (bi, i, 0)),
            pl.BlockSpec((1, n_sub, wqvt.shape[0], t_sub), lambda bi, i: (bi, i, 0, 0)),
        ],
        out_shape=[
            jax.ShapeDtypeStruct((b, n, wk.shape[1]), BF16),
            jax.ShapeDtypeStruct((b, n // t_sub, wqvt.shape[0], t_sub), BF16),
        ],
        compiler_params=pltpu.CompilerParams(
            dimension_semantics=("arbitrary", "arbitrary"), vmem_limit_bytes=VMEM_LIMIT),
        name="proj",
    )(x, g, wk, wqvt)


def _diff_kernel(lam_ref, q_ref, k_ref, v_ref, km_ref, vm_ref, bias_ref, bm_ref, dn_ref, o_ref,
                 m1, l1, a1, m2, l2, a2, s_buf, mx_buf, p_buf, alpha_buf):
    qi = pl.program_id(2)
    q = q_ref[0, 0]
    zq = jnp.zeros((HEAD_DIM, TQ), BF16)
    qa = jnp.concatenate([q[:HEAD_DIM], zq], axis=0)
    qb = jnp.concatenate([zq, q[HEAD_DIM:]], axis=0)

    for m_ref, l_ref, a_ref in ((m1, l1, a1), (m2, l2, a2)):
        m_ref[...] = jnp.full(m_ref.shape, NEG_INF, F32)
        l_ref[...] = jnp.zeros(l_ref.shape, F32)
        a_ref[...] = jnp.zeros(a_ref.shape, F32)

    def step(parts):
        for qq, m_ref, l_ref, a_ref in ((qa, m1, l1, a1), (qb, m2, l2, a2)):
            scores = []
            for kblk, _, bias in parts:
                s = jnp.dot(kblk, qq, preferred_element_type=F32)
                scores.append(s if bias is None else s + bias)
            m_old = m_ref[...]
            m_new = m_old
            for s in scores:
                m_new = jnp.maximum(m_new, jnp.max(s, axis=0, keepdims=True))
            alpha = jnp.exp2(m_old - m_new)
            l_new = alpha * l_ref[...]
            acc = alpha * a_ref[...]
            for (_, vts, _), s in zip(parts, scores):
                p = jnp.exp2(s - m_new)
                l_new = l_new + jnp.sum(p, axis=0, keepdims=True)
                pb = p.astype(BF16)
                off = 0
                for vt in vts:
                    w = vt.shape[1]
                    acc = acc + jnp.dot(vt, pb[off:off + w], preferred_element_type=F32)
                    off += w
            l_ref[...] = l_new
            a_ref[...] = acc
            m_ref[...] = m_new

    def k_rows(tile, n_tiles):
        return k_ref[0, pl.ds(pl.multiple_of(tile * TQ, TQ), n_tiles * TQ), :]

    t0 = jnp.maximum(qi - 1, 0)
    step([(km_ref[...], [vm_ref[...]], bm_ref[0, 0]),
          (k_rows(t0, 2), [v_ref[0, t0], v_ref[0, t0 + 1]], bias_ref[0, 0])])

    def scores_into(t, buf):
        kblk = k_rows(t, 1)
        for pi, qq in enumerate((qa, qb)):
            s = jnp.dot(kblk, qq, preferred_element_type=F32)
            s_buf[buf, pi] = s
            mx_buf[buf, pi] = jnp.max(s, axis=0, keepdims=True)

    def softmax_update(buf, pi, m_ref, l_ref, live=None):
        m_old = m_ref[...]
        m_new = jnp.maximum(m_old, mx_buf[buf, pi])
        p = jnp.exp2(s_buf[buf, pi] - m_new)
        if live is not None:
            m_new = jnp.where(live, m_new, m_old)
            p = jnp.where(live, p, 0.0)
        alpha = jnp.exp2(m_old - m_new)
        l_ref[...] = alpha * l_ref[...] + jnp.sum(p, axis=0, keepdims=True)
        m_ref[...] = m_new
        return alpha, p.astype(BF16)

    def update_from(t, buf, live=None):
        vt = v_ref[0, t]
        for pi, (m_ref, l_ref, a_ref) in enumerate(((m1, l1, a1), (m2, l2, a2))):
            alpha, pb = softmax_update(buf, pi, m_ref, l_ref, live)
            a_ref[...] = alpha * a_ref[...] + jnp.dot(vt, pb, preferred_element_type=F32)

    def update_deferred(buf):
        for pi, (m_ref, l_ref, _) in enumerate(((m1, l1, a1), (m2, l2, a2))):
            alpha_buf[pi], p_buf[pi] = softmax_update(buf, pi, m_ref, l_ref)

    def finish_deferred(t):
        vt = v_ref[0, t]
        for pi, a_ref in enumerate((a1, a2)):
            a_ref[...] = alpha_buf[pi] * a_ref[...] + jnp.dot(vt, p_buf[pi], preferred_element_type=F32)

    alpha_buf[...] = jnp.ones(alpha_buf.shape, F32)
    p_buf[...] = jnp.zeros(p_buf.shape, BF16)
    scores_into(0, 0)

    def far_pair(i, carry):
        t = 2 * i
        finish_deferred(jnp.maximum(t - 1, 0))
        scores_into(t + 1, 1)
        update_from(t, 0)
        scores_into(t + 2, 0)
        update_deferred(1)
        return carry

    n_pairs = t0 // 2
    lax.fori_loop(0, n_pairs, far_pair, 0)
    finish_deferred(jnp.maximum(2 * n_pairs - 1, 0))
    update_from(jnp.maximum(t0 - 1, 0), 0, live=(t0 & 1) == 1)

    lp = lam_ref[...]
    lam = (jnp.exp(jnp.sum(lp[0:1] * lp[1:2], axis=-1, keepdims=True))
           - jnp.exp(jnp.sum(lp[2:3] * lp[3:4], axis=-1, keepdims=True)) + LAMBDA_INIT)
    o = a1[...] / l1[...] - lam * (a2[...] / l2[...])
    y = o * lax.rsqrt(jnp.mean(o * o, axis=0, keepdims=True) + EPS) * dn_ref[...]
    o_ref[0] = (y * (1.0 - LAMBDA_INIT)).T.astype(BF16)


def _diff_call(lamp, kk, qvt, k_meta, vt_meta, bias_near, bias_meta, dn_col):
    b, n, _ = kk.shape
    nt = n // TQ
    return pl.pallas_call(
        _diff_kernel,
        grid=(b, N_DIFF_HEADS, nt),
        in_specs=[
            pl.BlockSpec(lamp.shape, lambda bi, h, qi: (0, 0)),
            pl.BlockSpec((1, 1, 128, TQ), lambda bi, h, qi: (bi, qi, h, 0)),
            pl.BlockSpec((1, n, 128), lambda bi, h, qi: (bi, 0, h)),
            pl.BlockSpec((1, nt, 128, TQ), lambda bi, h, qi: (bi, 0, N_DIFF_HEADS + h, 0)),
            pl.BlockSpec((N_META, 128), lambda bi, h, qi: (0, h)),
            pl.BlockSpec((128, N_META), lambda bi, h, qi: (N_DIFF_HEADS + h, 0)),
            pl.BlockSpec((1, 1, 2 * TQ, TQ), lambda bi, h, qi: (h, jnp.minimum(qi, 1), 0, 0)),
            pl.BlockSpec((1, 1, N_META, TQ), lambda bi, h, qi: (h, jnp.minimum(qi, 1), 0, 0)),
            pl.BlockSpec((DIFF_V_DIM, 1), lambda bi, h, qi: (0, 0)),
        ],
        out_specs=pl.BlockSpec((1, TQ, 128), lambda bi, h, qi: (bi, qi, h)),
        out_shape=jax.ShapeDtypeStruct((b, n, N_DIFF_HEADS * DIFF_V_DIM), BF16),
        scratch_shapes=[
            pltpu.VMEM((1, TQ), F32), pltpu.VMEM((1, TQ), F32), pltpu.VMEM((DIFF_V_DIM, TQ), F32),
            pltpu.VMEM((1, TQ), F32), pltpu.VMEM((1, TQ), F32), pltpu.VMEM((DIFF_V_DIM, TQ), F32),
            pltpu.VMEM((2, 2, TQ, TQ), F32), pltpu.VMEM((2, 2, 1, TQ), F32),
            pltpu.VMEM((2, TQ, TQ), BF16), pltpu.VMEM((2, 1, TQ), F32),
        ],
        compiler_params=pltpu.CompilerParams(
            dimension_semantics=("arbitrary", "arbitrary", "arbitrary"), vmem_limit_bytes=VMEM_LIMIT),
        name="diff_attn",
    )(lamp, qvt, kk, qvt, k_meta, vt_meta, bias_near, bias_meta, dn_col)


def _softplus(z):
    return jnp.maximum(z, 0.0) + jnp.log(1.0 + jnp.exp(-jnp.abs(z)))


def _sb_kernel(q_ref, k_ref, v_ref, km_ref, vm_ref, g_ref, o_ref, r_ref, a_ref):
    qi = pl.program_id(2)
    q = q_ref[0, 0]
    zq = jnp.zeros((HEAD_DIM, TQ), BF16)

    def q_padded(g):
        qg = q[g * HEAD_DIM:(g + 1) * HEAD_DIM]
        return jnp.concatenate([qg, zq] if g % 2 == 0 else [zq, qg], axis=0)

    qs = [q_padded(g) for g in range(SB_GROUP)]
    heads = [(g, slice(g * HEAD_DIM, (g + 1) * HEAD_DIM), slice((g // 2) * 128, (g // 2 + 1) * 128))
             for g in range(SB_GROUP)]

    def tri(n):
        return (lax.broadcasted_iota(I32, (n, n), 0) <= lax.broadcasted_iota(I32, (n, n), 1)).astype(BF16)

    def suffix_sums(t, sp):
        hi = sp.astype(BF16)
        lo = (sp - hi.astype(F32)).astype(BF16)
        return jnp.dot(t, hi, preferred_element_type=F32) + jnp.dot(t, lo, preferred_element_type=F32)

    def k_rows(tile, n_tiles):
        return k_ref[0, pl.ds(pl.multiple_of(tile * TQ, TQ), n_tiles * TQ), :]

    t0 = jnp.maximum(qi - 1, 0)
    has_prev = qi > 0
    k_prev, k_diag = k_rows(t0, 1), k_rows(qi, 1)
    v_prev, v_diag = v_ref[0, t0], v_ref[0, qi]
    keep = lax.broadcasted_iota(I32, (TQ, TQ), 0) < lax.broadcasted_iota(I32, (TQ, TQ), 1)
    t_full = tri(TQ)
    for g, rows, lanes in heads:
        z_diag = jnp.where(keep, jnp.dot(k_diag[:, lanes], qs[g], preferred_element_type=F32), NEG_INF)
        z_prev = jnp.dot(k_prev[:, lanes], qs[g], preferred_element_type=F32)
        suf_diag = suffix_sums(t_full, _softplus(z_diag))
        suf_prev = suffix_sums(t_full, _softplus(z_prev))
        tot_diag = suf_diag[0:1, :]
        a_diag = jnp.exp(z_diag - suf_diag).astype(BF16)
        a_prev = jnp.exp(z_prev - suf_prev - tot_diag).astype(BF16)
        acc_prev = jnp.dot(v_prev[rows, :], a_prev, preferred_element_type=F32)
        a_ref[rows, :] = (jnp.dot(v_diag[rows, :], a_diag, preferred_element_type=F32)
                          + jnp.where(has_prev, acc_prev, 0.0))
        r_ref[g:g + 1, :] = tot_diag + jnp.where(has_prev, suf_prev[0:1, :], 0.0)

    def step(kblk, vt):
        t = tri(kblk.shape[0])
        for g, rows, lanes in heads:
            z = jnp.dot(kblk[:, lanes], qs[g], preferred_element_type=F32)
            suf = suffix_sums(t, _softplus(z))
            r_old = r_ref[g:g + 1, :]
            a = jnp.exp(z - suf - r_old)
            a_ref[rows, :] += jnp.dot(vt[rows, :], a.astype(BF16), preferred_element_type=F32)
            r_ref[g:g + 1, :] = r_old + suf[0:1, :]

    def more(c):
        i, r_min = c
        return jnp.logical_and(i < t0, r_min < SB_UNDERFLOW)

    def left(c):
        i, _ = c
        kt = t0 - 1 - i
        step(k_rows(kt, 1), v_ref[0, kt])
        return i + 1, jnp.min(r_ref[...])

    _, r_min = lax.while_loop(more, left, (jnp.int32(0), jnp.min(r_ref[...])))

    @pl.when(r_min < SB_UNDERFLOW)
    def _():
        step(km_ref[...], vm_ref[...])

    ys = []
    for g, rows, _ in heads:
        o = a_ref[rows, :]
        ys.append(o * lax.rsqrt(jnp.mean(o * o, axis=0, keepdims=True) + EPS) * g_ref[...])
    o_ref[0] = jnp.concatenate(ys, axis=0).T.astype(BF16)


def _sb_call(kk, qvt, k_meta, vt_meta, g_col):
    b, n, _ = kk.shape
    nt = n // TQ
    w = SB_GROUP * HEAD_DIM
    n_groups = N_SB_HEADS // SB_GROUP
    k_off = N_DIFF_HEADS * 128 // w
    q_off = 2 * N_DIFF_HEADS * 128 // w
    v_off = 3 * N_DIFF_HEADS * 128 // w
    return pl.pallas_call(
        _sb_kernel,
        grid=(b, n_groups, nt),
        in_specs=[
            pl.BlockSpec((1, 1, w, TQ), lambda bi, h, qi: (bi, qi, q_off + h, 0)),
            pl.BlockSpec((1, n, w), lambda bi, h, qi: (bi, 0, k_off + h)),
            pl.BlockSpec((1, nt, w, TQ), lambda bi, h, qi: (bi, 0, v_off + h, 0)),
            pl.BlockSpec((N_META, w), lambda bi, h, qi: (0, k_off + h)),
            pl.BlockSpec((w, N_META), lambda bi, h, qi: (v_off + h, 0)),
            pl.BlockSpec((HEAD_DIM, 1), lambda bi, h, qi: (0, 0)),
        ],
        out_specs=pl.BlockSpec((1, TQ, w), lambda bi, h, qi: (bi, qi, h)),
        out_shape=jax.ShapeDtypeStruct((b, n, N_SB_HEADS * HEAD_DIM), BF16),
        scratch_shapes=[pltpu.VMEM((SB_GROUP, TQ), F32), pltpu.VMEM((w, TQ), F32)],
        compiler_params=pltpu.CompilerParams(
            dimension_semantics=("arbitrary", "arbitrary", "arbitrary"), vmem_limit_bytes=VMEM_LIMIT),
        name="sb_attn",
    )(qvt, kk, qvt, k_meta, vt_meta, g_col)


def _outproj_kernel(x_ref, ma_ref, ms_ref, wa_ref, ws_ref, fg_ref, wr_ref, h_ref, rc_ref):
    h1 = (x_ref[0] + jnp.dot(ma_ref[0], wa_ref[...], preferred_element_type=F32)
          + jnp.dot(ms_ref[0], ws_ref[...], preferred_element_type=F32))
    h_ref[0, :, :D_MODEL] = h1
    hn = _rms(h1, fg_ref[...])
    lg = lax.dot_general(wr_ref[...], hn, (((1,), (1,)), ((), ())),
                         precision=lax.Precision.HIGHEST, preferred_element_type=F32)
    t = lg.shape[1]
    rows = [lg[i:i + 1, :] for i in range(N_GROUPS + N_EXPERTS)]

    def first_argmax(vals, skip=None):
        best, idx = None, None
        for j, v in enumerate(vals):
            if skip is not None:
                v = jnp.where(skip == j, -jnp.inf, v)
            if best is None:
                best, idx = v, jnp.zeros((1, t), I32)
            else:
                better = v > best
                idx = jnp.where(better, j, idx)
                best = jnp.where(better, v, best)
        return best, idx

    g_best, g_idx = first_argmax(rows[:N_GROUPS])
    g_w = 1.0 / sum(jnp.exp(r - g_best) for r in rows[:N_GROUPS])
    el = []
    for j in range(EXPERTS_PER_GROUP):
        v = rows[N_GROUPS + j]
        for g in range(1, N_GROUPS):
            v = jnp.where(g_idx == g, rows[N_GROUPS + EXPERTS_PER_GROUP * g + j], v)
        el.append(v)
    e1, i1 = first_argmax(el)
    e2, i2 = first_argmax(el, skip=i1)
    p2 = jnp.exp(e2 - e1)
    w1 = g_w / (1.0 + p2)
    w2 = g_w * p2 / (1.0 + p2)
    lo = jnp.minimum(i1, i2)
    hi = jnp.maximum(i1, i2)
    pair = jnp.where(lo == 0, 0, jnp.where(lo == 1, 3, 5)) + (hi - lo - 1)
    cls = g_idx * len(PAIR_LO) + pair
    first_is_lo = i1 < i2
    w_lo = jnp.where(first_is_lo, w1, w2)
    w_hi = jnp.where(first_is_lo, w2, w1)
    h_ref[0, :, D_MODEL:] = jnp.concatenate([w_lo, w_hi, jnp.zeros((126, t), F32)], axis=0).T
    rc_ref[0, 0] = jnp.concatenate([cls, jnp.zeros((7, t), I32)], axis=0)


def _outproj_call(x, ma, ms, wa, ws, fg, wr, t_tok):
    b, n, d = x.shape
    nt = n // t_tok
    return pl.pallas_call(
        _outproj_kernel,
        grid=(b, nt),
        in_specs=[
            pl.BlockSpec((1, t_tok, d), lambda bi, i: (bi, i, 0)),
            pl.BlockSpec((1, t_tok, ma.shape[2]), lambda bi, i: (bi, i, 0)),
            pl.BlockSpec((1, t_tok, ms.shape[2]), lambda bi, i: (bi, i, 0)),
            pl.BlockSpec(wa.shape, lambda bi, i: (0, 0)),
            pl.BlockSpec(ws.shape, lambda bi, i: (0, 0)),
            pl.BlockSpec((1, d), lambda bi, i: (0, 0)),
            pl.BlockSpec(wr.shape, lambda bi, i: (0, 0)),
        ],
        out_specs=[
            pl.BlockSpec((1, t_tok, D_ROW), lambda bi, i: (bi, i, 0)),
            pl.BlockSpec((1, 1, 8, t_tok), lambda bi, i: (bi, i, 0, 0)),
        ],
        out_shape=[
            jax.ShapeDtypeStruct((b, n, D_ROW), F32),
            jax.ShapeDtypeStruct((b, nt, 8, t_tok), I32),
        ],
        compiler_params=pltpu.CompilerParams(
            dimension_semantics=("arbitrary", "arbitrary"), vmem_limit_bytes=VMEM_LIMIT),
        name="outproj_router",
    )(x, ma, ms, wa, ws, fg, wr)


def _moe_kernel(rows_ref, nval_ref, e1_ref, e2_ref, nused_ref,
                h_hbm, wgu1_ref, wd1_ref, wgu2_ref, wd2_ref, fg_ref, og_ref, out_hbm,
                xbuf, obuf, gsem, ssem):
    t = pl.program_id(0)
    n_used = nused_ref[0]
    slot = t & 1

    def row_in(tile, s):
        def make(r):
            tok = rows_ref[tile * T_MOE + r]
            return pltpu.make_async_copy(h_hbm.at[pl.ds(tok, 1), :], xbuf.at[s, pl.ds(r, 1), :], gsem.at[s])
        return make

    def row_out(tile, s):
        def make(r):
            tok = rows_ref[tile * T_MOE + r]
            return pltpu.make_async_copy(obuf.at[s, pl.ds(r, 1), :], out_hbm.at[pl.ds(tok, 1), :], ssem.at[s])
        return make

    def all_in(s):
        return pltpu.make_async_copy(h_hbm.at[pl.ds(0, T_MOE), :], xbuf.at[s], gsem.at[s])

    def all_out(s):
        return pltpu.make_async_copy(obuf.at[s], out_hbm.at[pl.ds(0, T_MOE), :], ssem.at[s])

    def for_rows(n, fn):
        def body(r, c):
            fn(r)
            return c
        lax.fori_loop(0, n, body, 0)

    def start_rows(tile, make):
        nv = nval_ref[tile]

        @pl.when(nv == T_MOE)
        def _():
            for r in range(T_MOE):
                make(r).start()

        @pl.when(nv < T_MOE)
        def _():
            for_rows(nv, lambda r: make(r).start())

    def wait_rows(tile, make, whole):
        nv = nval_ref[tile]

        @pl.when(nv == T_MOE)
        def _():
            whole.wait()

        @pl.when(nv < T_MOE)
        def _():
            for_rows(nv, lambda r: make(r).wait())

    @pl.when(t == 0)
    def _():
        xbuf[...] = jnp.zeros(xbuf.shape, F32)
        for_rows(nval_ref[0], lambda r: row_in(0, 0)(r).start())

    @pl.when(t + 1 < n_used)
    def _():
        start_rows(t + 1, row_in(t + 1, 1 - slot))

    @pl.when(t < n_used)
    def _():
        wait_rows(t, row_in(t, slot), all_in(slot))

        @pl.when(t >= 2)
        def _():
            wait_rows(t - 2, row_out(t - 2, slot), all_out(slot))

        xw = xbuf[slot]
        x = xw[:, :D_MODEL]
        w = xw[:, D_MODEL:]
        hn = _rms(x, fg_ref[...]).astype(BF16)
        y = jnp.zeros(x.shape, F32)
        for j, (wgu_ref, wd_ref) in enumerate(((wgu1_ref, wd1_ref), (wgu2_ref, wd2_ref))):
            gu = jnp.dot(hn, wgu_ref[0], preferred_element_type=F32)
            g = gu[:, :D_EXPERT]
            hd = g / (1.0 + jnp.exp(-g)) * gu[:, D_EXPERT:] * w[:, j:j + 1]
            y = y + jnp.dot(hd.astype(BF16), wd_ref[0], preferred_element_type=F32)
        obuf[slot] = _rms(x + y, og_ref[...])
        start_rows(t, row_out(t, slot))

        @pl.when(t == n_used - 1)
        def _():
            @pl.when(t >= 1)
            def _():
                wait_rows(t - 1, row_out(t - 1, 1 - slot), all_out(1 - slot))
            wait_rows(t, row_out(t, slot), all_out(slot))


def _moe_call(rows, nval, e1, e2, nused, h1, wgu, wd, fg, og):
    n_tok = h1.shape[0]
    d = D_MODEL
    n_tiles = rows.shape[0] // T_MOE
    grid_spec = pltpu.PrefetchScalarGridSpec(
        num_scalar_prefetch=5,
        grid=(n_tiles,),
        in_specs=[
            pl.BlockSpec(memory_space=pl.ANY),
            pl.BlockSpec((1, d, 2 * D_EXPERT), lambda t, rows, nval, e1, e2, nu: (e1[t], 0, 0)),
            pl.BlockSpec((1, D_EXPERT, d), lambda t, rows, nval, e1, e2, nu: (e1[t], 0, 0)),
            pl.BlockSpec((1, d, 2 * D_EXPERT), lambda t, rows, nval, e1, e2, nu: (e2[t], 0, 0)),
            pl.BlockSpec((1, D_EXPERT, d), lambda t, rows, nval, e1, e2, nu: (e2[t], 0, 0)),
            pl.BlockSpec((1, d), lambda t, *_: (0, 0)),
            pl.BlockSpec((1, d), lambda t, *_: (0, 0)),
        ],
        out_specs=pl.BlockSpec(memory_space=pl.ANY),
        scratch_shapes=[
            pltpu.VMEM((2, T_MOE, D_ROW), F32),
            pltpu.VMEM((2, T_MOE, d), F32),
            pltpu.SemaphoreType.DMA((2,)),
            pltpu.SemaphoreType.DMA((2,)),
        ],
    )
    return pl.pallas_call(
        _moe_kernel,
        grid_spec=grid_spec,
        out_shape=jax.ShapeDtypeStruct((n_tok, d), F32),
        compiler_params=pltpu.CompilerParams(
            dimension_semantics=("arbitrary",), vmem_limit_bytes=VMEM_LIMIT),
        name="moe",
    )(rows, nval, e1, e2, nused, h1, wgu, wd, wgu, wd, fg, og)


def _rel_bucket(rel):
    half = N_BUCKETS // 2
    max_exact = half // 2
    ret = jnp.where(rel > 0, half, 0)
    n = jnp.abs(rel)
    nf = jnp.maximum(n, 1).astype(F32)
    large = max_exact + (jnp.log(nf / max_exact) / math.log(MAX_DIST / max_exact)
                         * (half - max_exact)).astype(I32)
    large = jnp.minimum(large, half - 1)
    return ret + jnp.where(n < max_exact, n, large)


def _bias_tables(rel_table):
    table = rel_table.astype(F32).T

    def lookup(rel):
        onehot = _rel_bucket(rel)[None, ..., None] == jnp.arange(N_BUCKETS)
        return jnp.sum(jnp.where(onehot, table[:, None, None, :], 0.0), axis=-1)

    far = lookup(jnp.full((1, 1), -8 * MAX_DIST))
    kk = jnp.arange(TQ)[:, None]
    qq = jnp.arange(TQ)[None, :]
    prev = lookup(kk - TQ - qq) - far
    diag = jnp.where(kk // CHUNK <= qq // CHUNK, lookup(kk - qq) - far, NEG_INF)
    masked = jnp.full_like(diag, NEG_INF)
    near = jnp.stack([jnp.concatenate([diag, masked], axis=1),
                      jnp.concatenate([prev, diag], axis=1)], axis=1)
    mm = jnp.arange(N_META)[:, None]
    meta0 = lookup(mm - N_META - qq) - far
    meta = jnp.stack([meta0, jnp.zeros_like(meta0)], axis=1)
    return near * LOG2E, meta * LOG2E


def _route_tables(cls, n_tiles):
    n = cls.shape[0]
    order = jnp.argsort(cls, stable=True).astype(I32)
    counts = jnp.sum(cls[:, None] == jnp.arange(N_CLASSES)[None, :], axis=0).astype(I32)
    padded = (counts + T_MOE - 1) // T_MOE * T_MOE
    pend = jnp.cumsum(padded)
    pstart = pend - padded
    ustart = jnp.cumsum(counts) - counts
    n_used = pend[-1] // T_MOE
    tile = jnp.arange(n_tiles, dtype=I32)
    tcls = jnp.sum(pend[None, :] <= (tile * T_MOE)[:, None], axis=1).astype(I32)
    tcls = jnp.minimum(tcls, N_CLASSES - 1)
    last = tcls[jnp.maximum(n_used - 1, 0)]
    used = tile < n_used
    tcls = jnp.where(used, tcls, last)
    nval = jnp.where(used, jnp.clip(pstart[tcls] + counts[tcls] - tile * T_MOE, 0, T_MOE), 0).astype(I32)
    r = jnp.arange(T_MOE, dtype=I32)
    src = (ustart[tcls] + tile * T_MOE - pstart[tcls])[:, None] + r[None, :]
    valid = r[None, :] < nval[:, None]
    rows = jnp.where(valid, order[jnp.clip(src, 0, n - 1)], 0).reshape(-1)
    grp = tcls // len(PAIR_LO)
    pair = tcls % len(PAIR_LO)
    e1 = grp * EXPERTS_PER_GROUP + jnp.array(PAIR_LO, I32)[pair]
    e2 = grp * EXPERTS_PER_GROUP + jnp.array(PAIR_HI, I32)[pair]
    return rows, nval, e1.astype(I32), e2.astype(I32), n_used.astype(I32).reshape(1)


def kernel(x, meta_tokens, rel_table, attn_norm, w_in, lambda_q1, lambda_k1, lambda_q2, lambda_k2,
           diff_norm, sb_norm, w_out, ffn_norm, w_group, w_router, w_gate, w_up, w_down, final_norm):
    b, n, d = x.shape
    assert d == D_MODEL and attn_norm.shape[0] == 1
    t_tok = min(T_PROJ, n)
    assert n % t_tok == 0 and n % TQ == 0 and (b * n) % T_MOE == 0

    w = w_in[0]
    scale = HEAD_DIM ** -0.5
    dq, dk, dv, sq, sk, sv = [w[:, i * 512:(i + 1) * 512] for i in range(6)]
    wk = jnp.concatenate([dk, sk], axis=1).astype(BF16)
    wqvt = jnp.concatenate([dq * (scale * LOG2E), dv, sq * scale, sv], axis=1).T.astype(BF16)
    g_attn = attn_norm[0].reshape(1, d)

    kk, qvt = _proj_call(x, g_attn, wk, wqvt, t_tok, TQ)
    meta_pad = jnp.zeros((1, 128, d), x.dtype).at[0, :N_META].set(meta_tokens.astype(x.dtype))
    kk_m, qvt_m = _proj_call(meta_pad, g_attn, wk, wqvt, 128, 128)
    k_meta = kk_m[0, :N_META]
    vt_meta = qvt_m[0, 0, :, :N_META]

    bias_near, bias_meta = _bias_tables(rel_table)
    lamp = jnp.stack([lambda_q1[0], lambda_k1[0], lambda_q2[0], lambda_k2[0]]).astype(F32)
    mixed_a = _diff_call(lamp, kk, qvt, k_meta, vt_meta, bias_near, bias_meta,
                         diff_norm[0].astype(F32).reshape(DIFF_V_DIM, 1))
    mixed_s = _sb_call(kk, qvt, k_meta, vt_meta, sb_norm[0].astype(F32).reshape(HEAD_DIM, 1))

    wo = w_out[0].astype(BF16)
    wr = jnp.concatenate([w_group[0], w_router[0], jnp.zeros((d, 12), F32)], axis=1).T
    h1, rc = _outproj_call(x, mixed_a, mixed_s, wo[:512], wo[512:], ffn_norm[0].reshape(1, d), wr, t_tok)

    n_tok = b * n
    n_tiles = n_tok // T_MOE + N_CLASSES
    rows, nval, e1, e2, n_used = _route_tables(rc[:, :, 0, :].reshape(n_tok), n_tiles)

    wgu = jnp.concatenate([w_gate[0], w_up[0]], axis=2).astype(BF16)
    wd = w_down[0].astype(BF16)
    out = _moe_call(rows, nval, e1, e2, n_used, h1.reshape(n_tok, D_ROW), wgu, wd,
                    ffn_norm[0].reshape(1, d), final_norm.reshape(1, d))
    return out.reshape(b, n, d)
```

```python
import functools
import math

import jax
import jax.numpy as jnp
from jax import lax
from jax.experimental import pallas as pl
from jax.experimental.pallas import tpu as pltpu

F32 = jnp.float32
BF16 = jnp.bfloat16
I32 = jnp.int32

D_MODEL = 1024
CHUNK = 64
N_META = 16
HEAD_DIM = 64
N_DIFF_HEADS = 4
DIFF_V_DIM = 128
N_SB_HEADS = 8
N_BUCKETS = 32
MAX_DIST = 128
N_GROUPS = 4
EXPERTS_PER_GROUP = 4
N_EXPERTS = 16
D_EXPERT = 512
EPS = 1e-6
NEG_INF = -1e30
LAMBDA_INIT = 0.8 - 0.6 * math.exp(-0.3 * 0)
LOG2E = math.log2(math.e)

PAIR_LO = (0, 0, 0, 1, 1, 2)
PAIR_HI = (1, 2, 3, 2, 3, 3)
N_CLASSES = N_GROUPS * len(PAIR_LO)

TQ = 256
SB_GROUP = 4
DIFF_GROUP = 2
SB_UNDERFLOW = 110.0
T_PROJ = 512
T_MOE = 256
D_ROW = D_MODEL + 128
VMEM_LIMIT = 48 * 1024 * 1024


def _rms(x, g):
    return x * lax.rsqrt(jnp.mean(x * x, axis=-1, keepdims=True) + EPS) * g


def _proj_kernel(x_ref, g_ref, wk_ref, wqvt_ref, k_ref, qvt_ref, *, n_sub, t_sub):
    hn = _rms(x_ref[0], g_ref[...]).astype(BF16)
    k_ref[0] = jnp.dot(hn, wk_ref[...], preferred_element_type=F32).astype(BF16)
    n_rows = wqvt_ref.shape[0]
    for r in range(0, n_rows, 512):
        blk = lax.dot_general(wqvt_ref[r:r + 512, :], hn, (((1,), (1,)), ((), ())),
                              preferred_element_type=F32)
        for j in range(n_sub):
            qvt_ref[0, j, r:r + 512, :] = blk[:, j * t_sub:(j + 1) * t_sub].astype(BF16)


def _proj_call(x, g, wk, wqvt, t_tok, t_sub):
    b, n, d = x.shape
    n_sub = t_tok // t_sub
    kern = functools.partial(_proj_kernel, n_sub=n_sub, t_sub=t_sub)
    return pl.pallas_call(
        kern,
        grid=(b, n // t_tok),
        in_specs=[
            pl.BlockSpec((1, t_tok, d), lambda bi, i: (bi, i, 0)),
            pl.BlockSpec((1, d), lambda bi, i: (0, 0)),
            pl.BlockSpec(wk.shape, lambda bi, i: (0, 0)),
            pl.BlockSpec(wqvt.shape, lambda bi, i: (0, 0)),
        ],
        out_specs=[
            pl.BlockSpec((1, t_tok, wk.shape[1]), lambda bi, i: (bi, i, 0)),
            pl.BlockSpec((1, n_sub, wqvt.shape[0], t_sub), lambda bi, i: (bi, i, 0, 0)),
        ],
        out_shape=[
            jax.ShapeDtypeStruct((b, n, wk.shape[1]), BF16),
            jax.ShapeDtypeStruct((b, n // t_sub, wqvt.shape[0], t_sub), BF16),
        ],
        compiler_params=pltpu.CompilerParams(
            dimension_semantics=("arbitrary", "arbitrary"), vmem_limit_bytes=VMEM_LIMIT),
        name="proj",
    )(x, g, wk, wqvt)


def _diff_kernel(lam_ref, q_ref, k_ref, v_ref, km_ref, vm_ref, bias_ref, bm_ref, dn_ref, o_ref,
                 m_s, l_s, a_s, s_buf, mx_buf, p_buf, alpha_buf):
    qi = pl.program_id(2)
    q = q_ref[0, 0]
    zq = jnp.zeros((HEAD_DIM, TQ), BF16)

    chains = []
    for h in range(DIFF_GROUP):
        blk = slice(h * 128, (h + 1) * 128)
        q1, q2 = q[h * 128:h * 128 + HEAD_DIM], q[h * 128 + HEAD_DIM:(h + 1) * 128]
        chains.append((2 * h, h, blk, jnp.concatenate([q1, zq], axis=0)))
        chains.append((2 * h + 1, h, blk, jnp.concatenate([zq, q2], axis=0)))

    m_s[...] = jnp.full(m_s.shape, NEG_INF, F32)
    l_s[...] = jnp.zeros(l_s.shape, F32)
    a_s[...] = jnp.zeros(a_s.shape, F32)

    def step(parts):
        scores = [[jnp.dot(kblk[:, blk], qq, preferred_element_type=F32) + bias[h, 0]
                   for kblk, _, bias in parts] for _, h, blk, qq in chains]
        m_new = []
        for (c, _, _, _), sc in zip(chains, scores):
            m = m_s[c]
            for s in sc:
                m = jnp.maximum(m, jnp.max(s, axis=0, keepdims=True))
            m_new.append(m)
        probs = [[jnp.exp2(s - m) for s in sc] for sc, m in zip(scores, m_new)]
        for (c, _, blk, _), m, pr in zip(chains, m_new, probs):
            alpha = jnp.exp2(m_s[c] - m)
            l_new = alpha * l_s[c]
            acc = alpha * a_s[c]
            for (_, vts, _), p in zip(parts, pr):
                l_new = l_new + jnp.sum(p, axis=0, keepdims=True)
                pb = p.astype(BF16)
                off = 0
                for vt in vts:
                    w = vt.shape[1]
                    acc = acc + jnp.dot(vt[blk, :], pb[off:off + w], preferred_element_type=F32)
                    off += w
            l_s[c] = l_new
            a_s[c] = acc
            m_s[c] = m

    def k_rows(tile, n_tiles):
        return k_ref[0, pl.ds(pl.multiple_of(tile * TQ, TQ), n_tiles * TQ), :]

    t0 = jnp.maximum(qi - 1, 0)
    step([(km_ref[...], [vm_ref[...]], bm_ref),
          (k_rows(t0, 2), [v_ref[0, t0], v_ref[0, t0 + 1]], bias_ref)])

    def scores_into(t, buf):
        kblk = k_rows(t, 1)
        for c, _, blk, qq in chains:
            s = jnp.dot(kblk[:, blk], qq, preferred_element_type=F32)
            s_buf[buf, c] = s
            mx_buf[buf, c] = jnp.max(s, axis=0, keepdims=True)

    def softmax_update(buf, live=None):
        m_old = [m_s[c] for c, _, _, _ in chains]
        m_new = [jnp.maximum(m, mx_buf[buf, c]) for c, m in enumerate(m_old)]
        p = [jnp.exp2(s_buf[buf, c] - m) for c, m in enumerate(m_new)]
        if live is not None:
            m_new = [jnp.where(live, mn, mo) for mn, mo in zip(m_new, m_old)]
            p = [jnp.where(live, x, 0.0) for x in p]
        alpha = [jnp.exp2(mo - mn) for mo, mn in zip(m_old, m_new)]
        for c, (al, x, mn) in enumerate(zip(alpha, p, m_new)):
            l_s[c] = al * l_s[c] + jnp.sum(x, axis=0, keepdims=True)
            m_s[c] = mn
        return [(al, x.astype(BF16)) for al, x in zip(alpha, p)]

    def update_from(t, buf, live=None):
        vt = v_ref[0, t]
        for (c, _, blk, _), (alpha, pb) in zip(chains, softmax_update(buf, live)):
            a_s[c] = alpha * a_s[c] + jnp.dot(vt[blk, :], pb, preferred_element_type=F32)

    def update_deferred(buf):
        for c, (alpha, pb) in enumerate(softmax_update(buf)):
            alpha_buf[c], p_buf[c] = alpha, pb

    def finish_deferred(t):
        vt = v_ref[0, t]
        for c, _, blk, _ in chains:
            a_s[c] = alpha_buf[c] * a_s[c] + jnp.dot(vt[blk, :], p_buf[c], preferred_element_type=F32)

    alpha_buf[...] = jnp.ones(alpha_buf.shape, F32)
    p_buf[...] = jnp.zeros(p_buf.shape, BF16)
    scores_into(0, 0)

    def far_pair(t):
        finish_deferred(jnp.maximum(t - 1, 0))
        scores_into(t + 1, 1)
        update_from(t, 0)
        scores_into(t + 2, 0)
        update_deferred(1)

    def far_quad(i, carry):
        far_pair(4 * i)
        far_pair(4 * i + 2)
        return carry

    def far_rest(t, carry):
        far_pair(2 * t)
        return carry

    n_pairs = t0 // 2
    lax.fori_loop(0, n_pairs // 2, far_quad, 0)
    lax.fori_loop(n_pairs // 2 * 2, n_pairs, far_rest, 0)
    finish_deferred(jnp.maximum(2 * n_pairs - 1, 0))
    update_from(jnp.maximum(t0 - 1, 0), 0, live=(t0 & 1) == 1)

    lp = lam_ref[...]
    lam = (jnp.exp(jnp.sum(lp[0:1] * lp[1:2], axis=-1, keepdims=True))
           - jnp.exp(jnp.sum(lp[2:3] * lp[3:4], axis=-1, keepdims=True)) + LAMBDA_INIT)
    ys = []
    for h in range(DIFF_GROUP):
        o = a_s[2 * h] / l_s[2 * h] - lam * (a_s[2 * h + 1] / l_s[2 * h + 1])
        ys.append(o * lax.rsqrt(jnp.mean(o * o, axis=0, keepdims=True) + EPS) * dn_ref[...])
    o_ref[0] = (jnp.concatenate(ys, axis=0) * (1.0 - LAMBDA_INIT)).T.astype(BF16)


def _diff_call(lamp, kk, qvt, k_meta, vt_meta, bias_near, bias_meta, dn_col):
    b, n, _ = kk.shape
    nt = n // TQ
    g = DIFF_GROUP
    w = g * 128
    n_chains = 2 * g
    v_off = N_DIFF_HEADS // g
    return pl.pallas_call(
        _diff_kernel,
        grid=(b, N_DIFF_HEADS // g, nt),
        in_specs=[
            pl.BlockSpec(lamp.shape, lambda bi, h, qi: (0, 0)),
            pl.BlockSpec((1, 1, w, TQ), lambda bi, h, qi: (bi, qi, h, 0)),
            pl.BlockSpec((1, n, w), lambda bi, h, qi: (bi, 0, h)),
            pl.BlockSpec((1, nt, w, TQ), lambda bi, h, qi: (bi, 0, v_off + h, 0)),
            pl.BlockSpec((N_META, w), lambda bi, h, qi: (0, h)),
            pl.BlockSpec((w, N_META), lambda bi, h, qi: (v_off + h, 0)),
            pl.BlockSpec((g, 1, 2 * TQ, TQ), lambda bi, h, qi: (h, jnp.minimum(qi, 1), 0, 0)),
            pl.BlockSpec((g, 1, N_META, TQ), lambda bi, h, qi: (h, jnp.minimum(qi, 1), 0, 0)),
            pl.BlockSpec((DIFF_V_DIM, 1), lambda bi, h, qi: (0, 0)),
        ],
        out_specs=pl.BlockSpec((1, TQ, w), lambda bi, h, qi: (bi, qi, h)),
        out_shape=jax.ShapeDtypeStruct((b, n, N_DIFF_HEADS * DIFF_V_DIM), BF16),
        scratch_shapes=[
            pltpu.VMEM((n_chains, 1, TQ), F32), pltpu.VMEM((n_chains, 1, TQ), F32),
            pltpu.VMEM((n_chains, DIFF_V_DIM, TQ), F32),
            pltpu.VMEM((2, n_chains, TQ, TQ), F32), pltpu.VMEM((2, n_chains, 1, TQ), F32),
            pltpu.VMEM((n_chains, TQ, TQ), BF16), pltpu.VMEM((n_chains, 1, TQ), F32),
        ],
        compiler_params=pltpu.CompilerParams(
            dimension_semantics=("arbitrary", "arbitrary", "arbitrary"), vmem_limit_bytes=VMEM_LIMIT),
        name="diff_attn",
    )(lamp, qvt, kk, qvt, k_meta, vt_meta, bias_near, bias_meta, dn_col)


def _softplus(z):
    return jnp.maximum(z, 0.0) + jnp.log(1.0 + jnp.exp(-jnp.abs(z)))


def _sb_kernel(q_ref, k_ref, v_ref, km_ref, vm_ref, g_ref, o_ref, r_ref, a_ref):
    qi = pl.program_id(2)
    q = q_ref[0, 0]
    zq = jnp.zeros((HEAD_DIM, TQ), BF16)

    def q_padded(g):
        qg = q[g * HEAD_DIM:(g + 1) * HEAD_DIM]
        return jnp.concatenate([qg, zq] if g % 2 == 0 else [zq, qg], axis=0)

    qs = [q_padded(g) for g in range(SB_GROUP)]
    heads = [(g, slice(g * HEAD_DIM, (g + 1) * HEAD_DIM), slice((g // 2) * 128, (g // 2 + 1) * 128))
             for g in range(SB_GROUP)]

    def tri(n):
        return (lax.broadcasted_iota(I32, (n, n), 0) <= lax.broadcasted_iota(I32, (n, n), 1)).astype(BF16)

    def suffix_sums(t, sp):
        hi = sp.astype(BF16)
        lo = (sp - hi.astype(F32)).astype(BF16)
        return jnp.dot(t, hi, preferred_element_type=F32) + jnp.dot(t, lo, preferred_element_type=F32)

    def k_rows(tile, n_tiles):
        return k_ref[0, pl.ds(pl.multiple_of(tile * TQ, TQ), n_tiles * TQ), :]

    t0 = jnp.maximum(qi - 1, 0)
    has_prev = qi > 0
    k_prev, k_diag = k_rows(t0, 1), k_rows(qi, 1)
    v_prev, v_diag = v_ref[0, t0], v_ref[0, qi]
    keep = lax.broadcasted_iota(I32, (TQ, TQ), 0) < lax.broadcasted_iota(I32, (TQ, TQ), 1)
    t_full = tri(TQ)
    z_diag = [jnp.where(keep, jnp.dot(k_diag[:, lanes], qs[g], preferred_element_type=F32), NEG_INF)
              for g, _, lanes in heads]
    z_prev = [jnp.dot(k_prev[:, lanes], qs[g], preferred_element_type=F32) for g, _, lanes in heads]
    suf_diag = [suffix_sums(t_full, _softplus(z)) for z in z_diag]
    suf_prev = [suffix_sums(t_full, _softplus(z)) for z in z_prev]
    a_diag = [jnp.exp(z - s).astype(BF16) for z, s in zip(z_diag, suf_diag)]
    a_prev = [jnp.exp(z - s - sd[0:1, :]).astype(BF16) for z, s, sd in zip(z_prev, suf_prev, suf_diag)]
    for g, rows, _ in heads:
        acc_prev = jnp.dot(v_prev[rows, :], a_prev[g], preferred_element_type=F32)
        a_ref[rows, :] = (jnp.dot(v_diag[rows, :], a_diag[g], preferred_element_type=F32)
                          + jnp.where(has_prev, acc_prev, 0.0))
        r_ref[g:g + 1, :] = suf_diag[g][0:1, :] + jnp.where(has_prev, suf_prev[g][0:1, :], 0.0)

    def step(kblk, vt):
        t = tri(kblk.shape[0])
        for g, rows, lanes in heads:
            z = jnp.dot(kblk[:, lanes], qs[g], preferred_element_type=F32)
            suf = suffix_sums(t, _softplus(z))
            r_old = r_ref[g:g + 1, :]
            a = jnp.exp(z - suf - r_old)
            a_ref[rows, :] += jnp.dot(vt[rows, :], a.astype(BF16), preferred_element_type=F32)
            r_ref[g:g + 1, :] = r_old + suf[0:1, :]

    def more(c):
        i, r_min = c
        return jnp.logical_and(i < t0, r_min < SB_UNDERFLOW)

    def left(c):
        i, _ = c
        kt = t0 - 1 - i
        step(k_rows(kt, 1), v_ref[0, kt])
        return i + 1, jnp.min(r_ref[...])

    _, r_min = lax.while_loop(more, left, (jnp.int32(0), jnp.min(r_ref[...])))

    @pl.when(r_min < SB_UNDERFLOW)
    def _():
        step(km_ref[...], vm_ref[...])

    ys = []
    for g, rows, _ in heads:
        o = a_ref[rows, :]
        ys.append(o * lax.rsqrt(jnp.mean(o * o, axis=0, keepdims=True) + EPS) * g_ref[...])
    o_ref[0] = jnp.concatenate(ys, axis=0).T.astype(BF16)


def _sb_call(kk, qvt, k_meta, vt_meta, g_col):
    b, n, _ = kk.shape
    nt = n // TQ
    w = SB_GROUP * HEAD_DIM
    n_groups = N_SB_HEADS // SB_GROUP
    k_off = N_DIFF_HEADS * 128 // w
    q_off = 2 * N_DIFF_HEADS * 128 // w
    v_off = 3 * N_DIFF_HEADS * 128 // w
    return pl.pallas_call(
        _sb_kernel,
        grid=(b, n_groups, nt),
        in_specs=[
            pl.BlockSpec((1, 1, w, TQ), lambda bi, h, qi: (bi, qi, q_off + h, 0)),
            pl.BlockSpec((1, n, w), lambda bi, h, qi: (bi, 0, k_off + h)),
            pl.BlockSpec((1, nt, w, TQ), lambda bi, h, qi: (bi, 0, v_off + h, 0)),
            pl.BlockSpec((N_META, w), lambda bi, h, qi: (0, k_off + h)),
            pl.BlockSpec((w, N_META), lambda bi, h, qi: (v_off + h, 0)),
            pl.BlockSpec((HEAD_DIM, 1), lambda bi, h, qi: (0, 0)),
        ],
        out_specs=pl.BlockSpec((1, TQ, w), lambda bi, h, qi: (bi, qi, h)),
        out_shape=jax.ShapeDtypeStruct((b, n, N_SB_HEADS * HEAD_DIM), BF16),
        scratch_shapes=[pltpu.VMEM((SB_GROUP, TQ), F32), pltpu.VMEM((w, TQ), F32)],
        compiler_params=pltpu.CompilerParams(
            dimension_semantics=("arbitrary", "arbitrary", "arbitrary"), vmem_limit_bytes=VMEM_LIMIT),
        name="sb_attn",
    )(qvt, kk, qvt, k_meta, vt_meta, g_col)


def _outproj_kernel(x_ref, ma_ref, ms_ref, wa_ref, ws_ref, fg_ref, wr_ref, h_ref, rc_ref):
    h1 = (x_ref[0] + jnp.dot(ma_ref[0], wa_ref[...], preferred_element_type=F32)
          + jnp.dot(ms_ref[0], ws_ref[...], preferred_element_type=F32))
    h_ref[0, :, :D_MODEL] = h1
    hn = _rms(h1, fg_ref[...])
    hn_hi = hn.astype(BF16)
    hn_lo = (hn - hn_hi.astype(F32)).astype(BF16)
    nt_dims = (((1,), (1,)), ((), ()))
    by_hi = lax.dot_general(wr_ref[...], hn_hi, nt_dims, preferred_element_type=F32)
    lg = (by_hi[:32] + by_hi[32:]
          + lax.dot_general(wr_ref[:32, :], hn_lo, nt_dims, preferred_element_type=F32))
    t = lg.shape[1]
    rows = [lg[i:i + 1, :] for i in range(N_GROUPS + N_EXPERTS)]

    def first_argmax(vals, skip=None):
        best, idx = None, None
        for j, v in enumerate(vals):
            if skip is not None:
                v = jnp.where(skip == j, -jnp.inf, v)
            if best is None:
                best, idx = v, jnp.zeros((1, t), I32)
            else:
                better = v > best
                idx = jnp.where(better, j, idx)
                best = jnp.where(better, v, best)
        return best, idx

    g_best, g_idx = first_argmax(rows[:N_GROUPS])
    g_w = 1.0 / sum(jnp.exp(r - g_best) for r in rows[:N_GROUPS])
    el = []
    for j in range(EXPERTS_PER_GROUP):
        v = rows[N_GROUPS + j]
        for g in range(1, N_GROUPS):
            v = jnp.where(g_idx == g, rows[N_GROUPS + EXPERTS_PER_GROUP * g + j], v)
        el.append(v)
    e1, i1 = first_argmax(el)
    e2, i2 = first_argmax(el, skip=i1)
    p2 = jnp.exp(e2 - e1)
    w1 = g_w / (1.0 + p2)
    w2 = g_w * p2 / (1.0 + p2)
    lo = jnp.minimum(i1, i2)
    hi = jnp.maximum(i1, i2)
    pair = jnp.where(lo == 0, 0, jnp.where(lo == 1, 3, 5)) + (hi - lo - 1)
    cls = g_idx * len(PAIR_LO) + pair
    first_is_lo = i1 < i2
    w_lo = jnp.where(first_is_lo, w1, w2)
    w_hi = jnp.where(first_is_lo, w2, w1)
    h_ref[0, :, D_MODEL:] = jnp.concatenate([w_lo, w_hi, jnp.zeros((126, t), F32)], axis=0).T
    rc_ref[0, 0] = jnp.concatenate([cls, jnp.zeros((7, t), I32)], axis=0)


def _outproj_call(x, ma, ms, wa, ws, fg, wr, t_tok):
    b, n, d = x.shape
    nt = n // t_tok
    return pl.pallas_call(
        _outproj_kernel,
        grid=(b, nt),
        in_specs=[
            pl.BlockSpec((1, t_tok, d), lambda bi, i: (bi, i, 0)),
            pl.BlockSpec((1, t_tok, ma.shape[2]), lambda bi, i: (bi, i, 0)),
            pl.BlockSpec((1, t_tok, ms.shape[2]), lambda bi, i: (bi, i, 0)),
            pl.BlockSpec(wa.shape, lambda bi, i: (0, 0)),
            pl.BlockSpec(ws.shape, lambda bi, i: (0, 0)),
            pl.BlockSpec((1, d), lambda bi, i: (0, 0)),
            pl.BlockSpec(wr.shape, lambda bi, i: (0, 0)),
        ],
        out_specs=[
            pl.BlockSpec((1, t_tok, D_ROW), lambda bi, i: (bi, i, 0)),
            pl.BlockSpec((1, 1, 8, t_tok), lambda bi, i: (bi, i, 0, 0)),
        ],
        out_shape=[
            jax.ShapeDtypeStruct((b, n, D_ROW), F32),
            jax.ShapeDtypeStruct((b, nt, 8, t_tok), I32),
        ],
        compiler_params=pltpu.CompilerParams(
            dimension_semantics=("arbitrary", "arbitrary"), vmem_limit_bytes=VMEM_LIMIT),
        name="outproj_router",
    )(x, ma, ms, wa, ws, fg, wr)


def _moe_kernel(rows_ref, nval_ref, e1_ref, e2_ref, nused_ref,
                h_hbm, wgu1_ref, wd1_ref, wgu2_ref, wd2_ref, fg_ref, og_ref, out_hbm,
                xbuf, obuf, gsem, ssem):
    t = pl.program_id(0)
    n_used = nused_ref[0]
    slot = t & 1

    def row_in(tile, s):
        def make(r):
            tok = rows_ref[tile * T_MOE + r]
            return pltpu.make_async_copy(h_hbm.at[pl.ds(tok, 1), :], xbuf.at[s, pl.ds(r, 1), :], gsem.at[s])
        return make

    def row_out(tile, s):
        def make(r):
            tok = rows_ref[tile * T_MOE + r]
            return pltpu.make_async_copy(obuf.at[s, pl.ds(r, 1), :], out_hbm.at[pl.ds(tok, 1), :], ssem.at[s])
        return make

    def all_in(s):
        return pltpu.make_async_copy(h_hbm.at[pl.ds(0, T_MOE), :], xbuf.at[s], gsem.at[s])

    def all_out(s):
        return pltpu.make_async_copy(obuf.at[s], out_hbm.at[pl.ds(0, T_MOE), :], ssem.at[s])

    def for_rows(n, fn):
        def body(r, c):
            fn(r)
            return c
        lax.fori_loop(0, n, body, 0)

    def start_rows(tile, make):
        nv = nval_ref[tile]

        @pl.when(nv == T_MOE)
        def _():
            for r in range(T_MOE):
                make(r).start()

        @pl.when(nv < T_MOE)
        def _():
            for_rows(nv, lambda r: make(r).start())

    def wait_rows(tile, make, whole):
        nv = nval_ref[tile]

        @pl.when(nv == T_MOE)
        def _():
            whole.wait()

        @pl.when(nv < T_MOE)
        def _():
            for_rows(nv, lambda r: make(r).wait())

    @pl.when(t == 0)
    def _():
        xbuf[...] = jnp.zeros(xbuf.shape, F32)
        for_rows(nval_ref[0], lambda r: row_in(0, 0)(r).start())

    @pl.when(t + 1 < n_used)
    def _():
        start_rows(t + 1, row_in(t + 1, 1 - slot))

    @pl.when(t < n_used)
    def _():
        wait_rows(t, row_in(t, slot), all_in(slot))

        @pl.when(t >= 2)
        def _():
            wait_rows(t - 2, row_out(t - 2, slot), all_out(slot))

        xw = xbuf[slot]
        x = xw[:, :D_MODEL]
        w = xw[:, D_MODEL:]
        hn = _rms(x, fg_ref[...]).astype(BF16)
        y = jnp.zeros(x.shape, F32)
        for j, (wgu_ref, wd_ref) in enumerate(((wgu1_ref, wd1_ref), (wgu2_ref, wd2_ref))):
            gu = jnp.dot(hn, wgu_ref[0], preferred_element_type=F32)
            g = gu[:, :D_EXPERT]
            hd = g / (1.0 + jnp.exp(-g)) * gu[:, D_EXPERT:] * w[:, j:j + 1]
            y = y + jnp.dot(hd.astype(BF16), wd_ref[0], preferred_element_type=F32)
        obuf[slot] = _rms(x + y, og_ref[...])
        start_rows(t, row_out(t, slot))

        @pl.when(t == n_used - 1)
        def _():
            @pl.when(t >= 1)
            def _():
                wait_rows(t - 1, row_out(t - 1, 1 - slot), all_out(1 - slot))
            wait_rows(t, row_out(t, slot), all_out(slot))


def _moe_call(rows, nval, e1, e2, nused, h1, wgu, wd, fg, og):
    n_tok = h1.shape[0]
    d = D_MODEL
    n_tiles = rows.shape[0] // T_MOE
    grid_spec = pltpu.PrefetchScalarGridSpec(
        num_scalar_prefetch=5,
        grid=(n_tiles,),
        in_specs=[
            pl.BlockSpec(memory_space=pl.ANY),
            pl.BlockSpec((1, d, 2 * D_EXPERT), lambda t, rows, nval, e1, e2, nu: (e1[t], 0, 0)),
            pl.BlockSpec((1, D_EXPERT, d), lambda t, rows, nval, e1, e2, nu: (e1[t], 0, 0)),
            pl.BlockSpec((1, d, 2 * D_EXPERT), lambda t, rows, nval, e1, e2, nu: (e2[t], 0, 0)),
            pl.BlockSpec((1, D_EXPERT, d), lambda t, rows, nval, e1, e2, nu: (e2[t], 0, 0)),
            pl.BlockSpec((1, d), lambda t, *_: (0, 0)),
            pl.BlockSpec((1, d), lambda t, *_: (0, 0)),
        ],
        out_specs=pl.BlockSpec(memory_space=pl.ANY),
        scratch_shapes=[
            pltpu.VMEM((2, T_MOE, D_ROW), F32),
            pltpu.VMEM((2, T_MOE, d), F32),
            pltpu.SemaphoreType.DMA((2,)),
            pltpu.SemaphoreType.DMA((2,)),
        ],
    )
    return pl.pallas_call(
        _moe_kernel,
        grid_spec=grid_spec,
        out_shape=jax.ShapeDtypeStruct((n_tok, d), F32),
        compiler_params=pltpu.CompilerParams(
            dimension_semantics=("arbitrary",), vmem_limit_bytes=VMEM_LIMIT),
        name="moe",
    )(rows, nval, e1, e2, nused, h1, wgu, wd, wgu, wd, fg, og)


def _rel_bucket(rel):
    half = N_BUCKETS // 2
    max_exact = half // 2
    ret = jnp.where(rel > 0, half, 0)
    n = jnp.abs(rel)
    nf = jnp.maximum(n, 1).astype(F32)
    large = max_exact + (jnp.log(nf / max_exact) / math.log(MAX_DIST / max_exact)
                         * (half - max_exact)).astype(I32)
    large = jnp.minimum(large, half - 1)
    return ret + jnp.where(n < max_exact, n, large)


def _bias_tables(rel_table):
    table = rel_table.astype(F32)
    far = table[_rel_bucket(jnp.array(-8 * MAX_DIST))]
    n_rel = 3 * TQ - 1
    rel = jnp.arange(-(2 * TQ - 1), TQ)
    rev = (table[_rel_bucket(rel)] - far)[::-1].T

    def rows(starts):
        return jnp.stack([rev[:, s:s + TQ] for s in starts], axis=1)

    prev = rows([n_rel - TQ - k for k in range(TQ)])
    diag = rows([TQ - 1 - k for k in range(TQ)])
    meta0 = rows([TQ + N_META - 1 - m for m in range(N_META)])
    kk = jnp.arange(TQ)[:, None]
    qq = jnp.arange(TQ)[None, :]
    diag = jnp.where(kk // CHUNK <= qq // CHUNK, diag, NEG_INF)
    masked = jnp.full_like(diag, NEG_INF)
    near = jnp.stack([jnp.concatenate([diag, masked], axis=1),
                      jnp.concatenate([prev, diag], axis=1)], axis=1)
    meta = jnp.stack([meta0, jnp.zeros_like(meta0)], axis=1)
    return near * LOG2E, meta * LOG2E


def _route_tables(cls, n_tiles):
    n = cls.shape[0]
    order = jnp.argsort(cls, stable=True).astype(I32)
    counts = jnp.sum(cls[:, None] == jnp.arange(N_CLASSES)[None, :], axis=0).astype(I32)
    padded = (counts + T_MOE - 1) // T_MOE * T_MOE
    pend = jnp.cumsum(padded)
    pstart = pend - padded
    ustart = jnp.cumsum(counts) - counts
    n_used = pend[-1] // T_MOE
    tile = jnp.arange(n_tiles, dtype=I32)
    tcls = jnp.sum(pend[None, :] <= (tile * T_MOE)[:, None], axis=1).astype(I32)
    tcls = jnp.minimum(tcls, N_CLASSES - 1)
    last = tcls[jnp.maximum(n_used - 1, 0)]
    used = tile < n_used
    tcls = jnp.where(used, tcls, last)
    nval = jnp.where(used, jnp.clip(pstart[tcls] + counts[tcls] - tile * T_MOE, 0, T_MOE), 0).astype(I32)
    r = jnp.arange(T_MOE, dtype=I32)
    src = (ustart[tcls] + tile * T_MOE - pstart[tcls])[:, None] + r[None, :]
    valid = r[None, :] < nval[:, None]
    rows = jnp.where(valid, order[jnp.clip(src, 0, n - 1)], 0).reshape(-1)
    grp = tcls // len(PAIR_LO)
    pair = tcls % len(PAIR_LO)
    e1 = grp * EXPERTS_PER_GROUP + jnp.array(PAIR_LO, I32)[pair]
    e2 = grp * EXPERTS_PER_GROUP + jnp.array(PAIR_HI, I32)[pair]
    return rows, nval, e1.astype(I32), e2.astype(I32), n_used.astype(I32).reshape(1)


def kernel(x, meta_tokens, rel_table, attn_norm, w_in, lambda_q1, lambda_k1, lambda_q2, lambda_k2,
           diff_norm, sb_norm, w_out, ffn_norm, w_group, w_router, w_gate, w_up, w_down, final_norm):
    b, n, d = x.shape
    assert d == D_MODEL and attn_norm.shape[0] == 1
    t_tok = min(T_PROJ, n)
    assert n % t_tok == 0 and n % TQ == 0 and (b * n) % T_MOE == 0

    w = w_in[0]
    scale = HEAD_DIM ** -0.5
    dq, dk, dv, sq, sk, sv = [w[:, i * 512:(i + 1) * 512] for i in range(6)]
    wk = jnp.concatenate([dk, sk], axis=1).astype(BF16)
    wqvt = jnp.concatenate([dq * (scale * LOG2E), dv, sq * scale, sv], axis=1).T.astype(BF16)
    g_attn = attn_norm[0].reshape(1, d)

    kk, qvt = _proj_call(x, g_attn, wk, wqvt, t_tok, TQ)
    meta_pad = jnp.zeros((1, 128, d), x.dtype).at[0, :N_META].set(meta_tokens.astype(x.dtype))
    kk_m, qvt_m = _proj_call(meta_pad, g_attn, wk, wqvt, 128, 128)
    k_meta = kk_m[0, :N_META]
    vt_meta = qvt_m[0, 0, :, :N_META]

    bias_near, bias_meta = _bias_tables(rel_table)
    lamp = jnp.stack([lambda_q1[0], lambda_k1[0], lambda_q2[0], lambda_k2[0]]).astype(F32)
    mixed_a = _diff_call(lamp, kk, qvt, k_meta, vt_meta, bias_near, bias_meta,
                         diff_norm[0].astype(F32).reshape(DIFF_V_DIM, 1))
    mixed_s = _sb_call(kk, qvt, k_meta, vt_meta, sb_norm[0].astype(F32).reshape(HEAD_DIM, 1))

    wo = w_out[0].astype(BF16)
    wr = jnp.concatenate([w_group[0], w_router[0], jnp.zeros((d, 12), F32)], axis=1).T
    wr_hi = wr.astype(BF16)
    wr = jnp.concatenate([wr_hi, (wr - wr_hi.astype(F32)).astype(BF16)], axis=0)
    h1, rc = _outproj_call(x, mixed_a, mixed_s, wo[:512], wo[512:], ffn_norm[0].reshape(1, d), wr, t_tok)

    n_tok = b * n
    n_tiles = n_tok // T_MOE + N_CLASSES
    rows, nval, e1, e2, n_used = _route_tables(rc[:, :, 0, :].reshape(n_tok), n_tiles)

    wgu = jnp.concatenate([w_gate[0], w_up[0]], axis=2).astype(BF16)
    wd = w_down[0].astype(BF16)
    out = _moe_call(rows, nval, e1, e2, n_used, h1.reshape(n_tok, D_ROW), wgu, wd,
                    ffn_norm[0].reshape(1, d), final_norm.reshape(1, d))
    return out.reshape(b, n, d)
```

```python
import functools
import math

import jax
import jax.numpy as jnp
from jax import lax
from jax.experimental import pallas as pl
from jax.experimental.pallas import tpu as pltpu

F32 = jnp.float32
BF16 = jnp.bfloat16
I32 = jnp.int32

D_MODEL = 1024
CHUNK = 64
N_META = 16
HEAD_DIM = 64
N_DIFF_HEADS = 4
DIFF_V_DIM = 128
N_SB_HEADS = 8
N_BUCKETS = 32
MAX_DIST = 128
N_GROUPS = 4
EXPERTS_PER_GROUP = 4
N_EXPERTS = 16
D_EXPERT = 512
EPS = 1e-6
NEG_INF = -1e30
LAMBDA_INIT = 0.8 - 0.6 * math.exp(-0.3 * 0)
LOG2E = math.log2(math.e)

PAIR_LO = (0, 0, 0, 1, 1, 2)
PAIR_HI = (1, 2, 3, 2, 3, 3)
N_CLASSES = N_GROUPS * len(PAIR_LO)

TQ = 256
SB_GROUP = 4
DIFF_GROUP = 2
SB_UNDERFLOW = 110.0
T_PROJ = 512
T_MOE = 256
D_ROW = D_MODEL + 128
VMEM_LIMIT = 48 * 1024 * 1024


def _rms(x, g):
    return x * lax.rsqrt(jnp.mean(x * x, axis=-1, keepdims=True) + EPS) * g


def _proj_kernel(x_ref, g_ref, wk_ref, wqvt_ref, k_ref, qvt_ref, *, n_sub, t_sub):
    hn = _rms(x_ref[0], g_ref[...]).astype(BF16)
    k_ref[0] = jnp.dot(hn, wk_ref[...], preferred_element_type=F32).astype(BF16)
    n_rows = wqvt_ref.shape[0]
    for r in range(0, n_rows, 512):
        blk = lax.dot_general(wqvt_ref[r:r + 512, :], hn, (((1,), (1,)), ((), ())),
                              preferred_element_type=F32)
        for j in range(n_sub):
            qvt_ref[0, j, r:r + 512, :] = blk[:, j * t_sub:(j + 1) * t_sub].astype(BF16)


def _proj_call(x, g, wk, wqvt, t_tok, t_sub):
    b, n, d = x.shape
    n_sub = t_tok // t_sub
    kern = functools.partial(_proj_kernel, n_sub=n_sub, t_sub=t_sub)
    return pl.pallas_call(
        kern,
        grid=(b, n // t_tok),
        in_specs=[
            pl.BlockSpec((1, t_tok, d), lambda bi, i: (bi, i, 0)),
            pl.BlockSpec((1, d), lambda bi, i: (0, 0)),
            pl.BlockSpec(wk.shape, lambda bi, i: (0, 0)),
            pl.BlockSpec(wqvt.shape, lambda bi, i: (0, 0)),
        ],
        out_specs=[
            pl.BlockSpec((1, t_tok, wk.shape[1]), lambda bi, i: (bi, i, 0)),
            pl.BlockSpec((1, n_sub, wqvt.shape[0], t_sub), lambda bi, i: (bi, i, 0, 0)),
        ],
        out_shape=[
            jax.ShapeDtypeStruct((b, n, wk.shape[1]), BF16),
            jax.ShapeDtypeStruct((b, n // t_sub, wqvt.shape[0], t_sub), BF16),
        ],
        compiler_params=pltpu.CompilerParams(
            dimension_semantics=("arbitrary", "arbitrary"), vmem_limit_bytes=VMEM_LIMIT),
        name="proj",
    )(x, g, wk, wqvt)


def _diff_kernel(lam_ref, q_ref, k_ref, v_ref, km_ref, vm_ref, bias_ref, bm_ref, dn_ref, o_ref,
                 m_s, l_s, a_s, s_buf, mx_buf, p_buf, alpha_buf):
    qi = pl.program_id(2)
    q = q_ref[0, 0]
    zq = jnp.zeros((HEAD_DIM, TQ), BF16)

    chains = []
    for h in range(DIFF_GROUP):
        blk = slice(h * 128, (h + 1) * 128)
        q1, q2 = q[h * 128:h * 128 + HEAD_DIM], q[h * 128 + HEAD_DIM:(h + 1) * 128]
        chains.append((2 * h, h, blk, jnp.concatenate([q1, zq], axis=0)))
        chains.append((2 * h + 1, h, blk, jnp.concatenate([zq, q2], axis=0)))

    m_s[...] = jnp.full(m_s.shape, NEG_INF, F32)
    l_s[...] = jnp.zeros(l_s.shape, F32)
    a_s[...] = jnp.zeros(a_s.shape, F32)

    def step(parts):
        scores = [[jnp.dot(kblk[:, blk], qq, preferred_element_type=F32) + bias[h, 0]
                   for kblk, _, bias in parts] for _, h, blk, qq in chains]
        m_new = []
        for (c, _, _, _), sc in zip(chains, scores):
            m = m_s[c]
            for s in sc:
                m = jnp.maximum(m, jnp.max(s, axis=0, keepdims=True))
            m_new.append(m)
        probs = [[jnp.exp2(s - m) for s in sc] for sc, m in zip(scores, m_new)]
        for (c, _, blk, _), m, pr in zip(chains, m_new, probs):
            alpha = jnp.exp2(m_s[c] - m)
            l_new = alpha * l_s[c]
            acc = alpha * a_s[c]
            for (_, vts, _), p in zip(parts, pr):
                l_new = l_new + jnp.sum(p, axis=0, keepdims=True)
                pb = p.astype(BF16)
                off = 0
                for vt in vts:
                    w = vt.shape[1]
                    acc = acc + jnp.dot(vt[blk, :], pb[off:off + w], preferred_element_type=F32)
                    off += w
            l_s[c] = l_new
            a_s[c] = acc
            m_s[c] = m

    def k_rows(tile, n_tiles):
        return k_ref[0, pl.ds(pl.multiple_of(tile * TQ, TQ), n_tiles * TQ), :]

    t0 = jnp.maximum(qi - 1, 0)
    step([(km_ref[...], [vm_ref[...]], bm_ref),
          (k_rows(t0, 2), [v_ref[0, t0], v_ref[0, t0 + 1]], bias_ref)])

    def scores_into(t, buf):
        kblk = k_rows(t, 1)
        for c, _, blk, qq in chains:
            s = jnp.dot(kblk[:, blk], qq, preferred_element_type=F32)
            s_buf[buf, c] = s
            mx_buf[buf, c] = jnp.max(s, axis=0, keepdims=True)

    def softmax_update(buf, live=None):
        m_old = [m_s[c] for c, _, _, _ in chains]
        m_new = [jnp.maximum(m, mx_buf[buf, c]) for c, m in enumerate(m_old)]
        p = [jnp.exp2(s_buf[buf, c] - m) for c, m in enumerate(m_new)]
        if live is not None:
            m_new = [jnp.where(live, mn, mo) for mn, mo in zip(m_new, m_old)]
            p = [jnp.where(live, x, 0.0) for x in p]
        alpha = [jnp.exp2(mo - mn) for mo, mn in zip(m_old, m_new)]
        for c, (al, x, mn) in enumerate(zip(alpha, p, m_new)):
            l_s[c] = al * l_s[c] + jnp.sum(x, axis=0, keepdims=True)
            m_s[c] = mn
        return [(al, x.astype(BF16)) for al, x in zip(alpha, p)]

    def update_from(t, buf, live=None):
        vt = v_ref[0, t]
        for (c, _, blk, _), (alpha, pb) in zip(chains, softmax_update(buf, live)):
            a_s[c] = alpha * a_s[c] + jnp.dot(vt[blk, :], pb, preferred_element_type=F32)

    def update_deferred(buf):
        for c, (alpha, pb) in enumerate(softmax_update(buf)):
            alpha_buf[c], p_buf[c] = alpha, pb

    def finish_deferred(t):
        vt = v_ref[0, t]
        for c, _, blk, _ in chains:
            a_s[c] = alpha_buf[c] * a_s[c] + jnp.dot(vt[blk, :], p_buf[c], preferred_element_type=F32)

    alpha_buf[...] = jnp.ones(alpha_buf.shape, F32)
    p_buf[...] = jnp.zeros(p_buf.shape, BF16)
    scores_into(0, 0)

    def far_pair(t):
        finish_deferred(jnp.maximum(t - 1, 0))
        scores_into(t + 1, 1)
        update_from(t, 0)
        scores_into(t + 2, 0)
        update_deferred(1)

    def far_quad(i, carry):
        far_pair(4 * i)
        far_pair(4 * i + 2)
        return carry

    def far_rest(t, carry):
        far_pair(2 * t)
        return carry

    n_pairs = t0 // 2
    lax.fori_loop(0, n_pairs // 2, far_quad, 0)
    lax.fori_loop(n_pairs // 2 * 2, n_pairs, far_rest, 0)
    finish_deferred(jnp.maximum(2 * n_pairs - 1, 0))
    update_from(jnp.maximum(t0 - 1, 0), 0, live=(t0 & 1) == 1)

    lp = lam_ref[...]
    lam = (jnp.exp(jnp.sum(lp[0:1] * lp[1:2], axis=-1, keepdims=True))
           - jnp.exp(jnp.sum(lp[2:3] * lp[3:4], axis=-1, keepdims=True)) + LAMBDA_INIT)
    ys = []
    for h in range(DIFF_GROUP):
        o = a_s[2 * h] / l_s[2 * h] - lam * (a_s[2 * h + 1] / l_s[2 * h + 1])
        ys.append(o * lax.rsqrt(jnp.mean(o * o, axis=0, keepdims=True) + EPS) * dn_ref[...])
    o_ref[0] = (jnp.concatenate(ys, axis=0) * (1.0 - LAMBDA_INIT)).T.astype(BF16)


def _diff_call(lamp, kk, qvt, k_meta, vt_meta, bias_near, bias_meta, dn_col):
    b, n, _ = kk.shape
    nt = n // TQ
    g = DIFF_GROUP
    w = g * 128
    n_chains = 2 * g
    v_off = N_DIFF_HEADS // g
    return pl.pallas_call(
        _diff_kernel,
        grid=(b, N_DIFF_HEADS // g, nt),
        in_specs=[
            pl.BlockSpec(lamp.shape, lambda bi, h, qi: (0, 0)),
            pl.BlockSpec((1, 1, w, TQ), lambda bi, h, qi: (bi, qi, h, 0)),
            pl.BlockSpec((1, n, w), lambda bi, h, qi: (bi, 0, h)),
            pl.BlockSpec((1, nt, w, TQ), lambda bi, h, qi: (bi, 0, v_off + h, 0)),
            pl.BlockSpec((N_META, w), lambda bi, h, qi: (0, h)),
            pl.BlockSpec((w, N_META), lambda bi, h, qi: (v_off + h, 0)),
            pl.BlockSpec((g, 1, 2 * TQ, TQ), lambda bi, h, qi: (h, jnp.minimum(qi, 1), 0, 0)),
            pl.BlockSpec((g, 1, N_META, TQ), lambda bi, h, qi: (h, jnp.minimum(qi, 1), 0, 0)),
            pl.BlockSpec((DIFF_V_DIM, 1), lambda bi, h, qi: (0, 0)),
        ],
        out_specs=pl.BlockSpec((1, TQ, w), lambda bi, h, qi: (bi, qi, h)),
        out_shape=jax.ShapeDtypeStruct((b, n, N_DIFF_HEADS * DIFF_V_DIM), BF16),
        scratch_shapes=[
            pltpu.VMEM((n_chains, 1, TQ), F32), pltpu.VMEM((n_chains, 1, TQ), F32),
            pltpu.VMEM((n_chains, DIFF_V_DIM, TQ), F32),
            pltpu.VMEM((2, n_chains, TQ, TQ), F32), pltpu.VMEM((2, n_chains, 1, TQ), F32),
            pltpu.VMEM((n_chains, TQ, TQ), BF16), pltpu.VMEM((n_chains, 1, TQ), F32),
        ],
        compiler_params=pltpu.CompilerParams(
            dimension_semantics=("arbitrary", "arbitrary", "arbitrary"), vmem_limit_bytes=VMEM_LIMIT),
        name="diff_attn",
    )(lamp, qvt, kk, qvt, k_meta, vt_meta, bias_near, bias_meta, dn_col)


def _softplus(z):
    return jnp.maximum(z, 0.0) + jnp.log(1.0 + jnp.exp(-jnp.abs(z)))


def _sb_kernel(q_ref, k_ref, v_ref, km_ref, vm_ref, g_ref, o_ref, r_ref, a_ref):
    qi = pl.program_id(2)
    q = q_ref[0, 0]
    zq = jnp.zeros((HEAD_DIM, TQ), BF16)

    def q_padded(g):
        qg = q[g * HEAD_DIM:(g + 1) * HEAD_DIM]
        return jnp.concatenate([qg, zq] if g % 2 == 0 else [zq, qg], axis=0)

    qs = [q_padded(g) for g in range(SB_GROUP)]
    heads = [(g, slice(g * HEAD_DIM, (g + 1) * HEAD_DIM), slice((g // 2) * 128, (g // 2 + 1) * 128))
             for g in range(SB_GROUP)]

    def tri(n):
        return (lax.broadcasted_iota(I32, (n, n), 0) <= lax.broadcasted_iota(I32, (n, n), 1)).astype(BF16)

    def suffix_sums(t, sp):
        hi = sp.astype(BF16)
        lo = (sp - hi.astype(F32)).astype(BF16)
        return jnp.dot(t, hi, preferred_element_type=F32) + jnp.dot(t, lo, preferred_element_type=F32)

    def k_rows(tile, n_tiles):
        return k_ref[0, pl.ds(pl.multiple_of(tile * TQ, TQ), n_tiles * TQ), :]

    t0 = jnp.maximum(qi - 1, 0)
    has_prev = qi > 0
    k_prev, k_diag = k_rows(t0, 1), k_rows(qi, 1)
    v_prev, v_diag = v_ref[0, t0], v_ref[0, qi]
    keep = lax.broadcasted_iota(I32, (TQ, TQ), 0) < lax.broadcasted_iota(I32, (TQ, TQ), 1)
    t_full = tri(TQ)
    z_diag = [jnp.where(keep, jnp.dot(k_diag[:, lanes], qs[g], preferred_element_type=F32), NEG_INF)
              for g, _, lanes in heads]
    z_prev = [jnp.dot(k_prev[:, lanes], qs[g], preferred_element_type=F32) for g, _, lanes in heads]
    suf_diag = [suffix_sums(t_full, _softplus(z)) for z in z_diag]
    suf_prev = [suffix_sums(t_full, _softplus(z)) for z in z_prev]
    a_diag = [jnp.exp(z - s).astype(BF16) for z, s in zip(z_diag, suf_diag)]
    a_prev = [jnp.exp(z - s - sd[0:1, :]).astype(BF16) for z, s, sd in zip(z_prev, suf_prev, suf_diag)]
    for g, rows, _ in heads:
        acc_prev = jnp.dot(v_prev[rows, :], a_prev[g], preferred_element_type=F32)
        a_ref[rows, :] = (jnp.dot(v_diag[rows, :], a_diag[g], preferred_element_type=F32)
                          + jnp.where(has_prev, acc_prev, 0.0))
        r_ref[g:g + 1, :] = suf_diag[g][0:1, :] + jnp.where(has_prev, suf_prev[g][0:1, :], 0.0)

    def step(kblk, vt):
        t = tri(kblk.shape[0])
        for g, rows, lanes in heads:
            z = jnp.dot(kblk[:, lanes], qs[g], preferred_element_type=F32)
            suf = suffix_sums(t, _softplus(z))
            r_old = r_ref[g:g + 1, :]
            a = jnp.exp(z - suf - r_old)
            a_ref[rows, :] += jnp.dot(vt[rows, :], a.astype(BF16), preferred_element_type=F32)
            r_ref[g:g + 1, :] = r_old + suf[0:1, :]

    def more(c):
        i, r_min = c
        return jnp.logical_and(i < t0, r_min < SB_UNDERFLOW)

    def left(c):
        i, _ = c
        kt = t0 - 1 - i
        step(k_rows(kt, 1), v_ref[0, kt])
        return i + 1, jnp.min(r_ref[...])

    _, r_min = lax.while_loop(more, left, (jnp.int32(0), jnp.min(r_ref[...])))

    @pl.when(r_min < SB_UNDERFLOW)
    def _():
        step(km_ref[...], vm_ref[...])

    ys = []
    for g, rows, _ in heads:
        o = a_ref[rows, :]
        ys.append(o * lax.rsqrt(jnp.mean(o * o, axis=0, keepdims=True) + EPS) * g_ref[...])
    o_ref[0] = jnp.concatenate(ys, axis=0).T.astype(BF16)


def _sb_call(kk, qvt, k_meta, vt_meta, g_col):
    b, n, _ = kk.shape
    nt = n // TQ
    w = SB_GROUP * HEAD_DIM
    n_groups = N_SB_HEADS // SB_GROUP
    k_off = N_DIFF_HEADS * 128 // w
    q_off = 2 * N_DIFF_HEADS * 128 // w
    v_off = 3 * N_DIFF_HEADS * 128 // w
    return pl.pallas_call(
        _sb_kernel,
        grid=(b, n_groups, nt),
        in_specs=[
            pl.BlockSpec((1, 1, w, TQ), lambda bi, h, qi: (bi, qi, q_off + h, 0)),
            pl.BlockSpec((1, n, w), lambda bi, h, qi: (bi, 0, k_off + h)),
            pl.BlockSpec((1, nt, w, TQ), lambda bi, h, qi: (bi, 0, v_off + h, 0)),
            pl.BlockSpec((N_META, w), lambda bi, h, qi: (0, k_off + h)),
            pl.BlockSpec((w, N_META), lambda bi, h, qi: (v_off + h, 0)),
            pl.BlockSpec((HEAD_DIM, 1), lambda bi, h, qi: (0, 0)),
        ],
        out_specs=pl.BlockSpec((1, TQ, w), lambda bi, h, qi: (bi, qi, h)),
        out_shape=jax.ShapeDtypeStruct((b, n, N_SB_HEADS * HEAD_DIM), BF16),
        scratch_shapes=[pltpu.VMEM((SB_GROUP, TQ), F32), pltpu.VMEM((w, TQ), F32)],
        compiler_params=pltpu.CompilerParams(
            dimension_semantics=("arbitrary", "arbitrary", "arbitrary"), vmem_limit_bytes=VMEM_LIMIT),
        name="sb_attn",
    )(qvt, kk, qvt, k_meta, vt_meta, g_col)


def _outproj_kernel(x_ref, ma_ref, ms_ref, wa_ref, ws_ref, fg_ref, wr_ref, h_ref, rc_ref):
    h1 = (x_ref[0] + jnp.dot(ma_ref[0], wa_ref[...], preferred_element_type=F32)
          + jnp.dot(ms_ref[0], ws_ref[...], preferred_element_type=F32))
    h_ref[0, :, :D_MODEL] = h1
    hn = _rms(h1, fg_ref[...])
    hn_hi = hn.astype(BF16)
    hn_lo = (hn - hn_hi.astype(F32)).astype(BF16)
    nt_dims = (((1,), (1,)), ((), ()))
    by_hi = lax.dot_general(wr_ref[...], hn_hi, nt_dims, preferred_element_type=F32)
    lg = (by_hi[:32] + by_hi[32:]
          + lax.dot_general(wr_ref[:32, :], hn_lo, nt_dims, preferred_element_type=F32))
    t = lg.shape[1]
    rows = [lg[i:i + 1, :] for i in range(N_GROUPS + N_EXPERTS)]

    def first_argmax(vals, skip=None):
        best, idx = None, None
        for j, v in enumerate(vals):
            if skip is not None:
                v = jnp.where(skip == j, -jnp.inf, v)
            if best is None:
                best, idx = v, jnp.zeros((1, t), I32)
            else:
                better = v > best
                idx = jnp.where(better, j, idx)
                best = jnp.where(better, v, best)
        return best, idx

    g_best, g_idx = first_argmax(rows[:N_GROUPS])
    g_w = 1.0 / sum(jnp.exp(r - g_best) for r in rows[:N_GROUPS])
    el = []
    for j in range(EXPERTS_PER_GROUP):
        v = rows[N_GROUPS + j]
        for g in range(1, N_GROUPS):
            v = jnp.where(g_idx == g, rows[N_GROUPS + EXPERTS_PER_GROUP * g + j], v)
        el.append(v)
    e1, i1 = first_argmax(el)
    e2, i2 = first_argmax(el, skip=i1)
    p2 = jnp.exp(e2 - e1)
    w1 = g_w / (1.0 + p2)
    w2 = g_w * p2 / (1.0 + p2)
    lo = jnp.minimum(i1, i2)
    hi = jnp.maximum(i1, i2)
    pair = jnp.where(lo == 0, 0, jnp.where(lo == 1, 3, 5)) + (hi - lo - 1)
    cls = g_idx * len(PAIR_LO) + pair
    first_is_lo = i1 < i2
    w_lo = jnp.where(first_is_lo, w1, w2)
    w_hi = jnp.where(first_is_lo, w2, w1)
    h_ref[0, :, D_MODEL:] = jnp.concatenate([w_lo, w_hi, jnp.zeros((126, t), F32)], axis=0).T
    rc_ref[0, 0] = jnp.concatenate([cls, jnp.zeros((7, t), I32)], axis=0)


def _outproj_call(x, ma, ms, wa, ws, fg, wr, t_tok):
    b, n, d = x.shape
    nt = n // t_tok
    return pl.pallas_call(
        _outproj_kernel,
        grid=(b, nt),
        in_specs=[
            pl.BlockSpec((1, t_tok, d), lambda bi, i: (bi, i, 0)),
            pl.BlockSpec((1, t_tok, ma.shape[2]), lambda bi, i: (bi, i, 0)),
            pl.BlockSpec((1, t_tok, ms.shape[2]), lambda bi, i: (bi, i, 0)),
            pl.BlockSpec(wa.shape, lambda bi, i: (0, 0)),
            pl.BlockSpec(ws.shape, lambda bi, i: (0, 0)),
            pl.BlockSpec((1, d), lambda bi, i: (0, 0)),
            pl.BlockSpec(wr.shape, lambda bi, i: (0, 0)),
        ],
        out_specs=[
            pl.BlockSpec((1, t_tok, D_ROW), lambda bi, i: (bi, i, 0)),
            pl.BlockSpec((1, 1, 8, t_tok), lambda bi, i: (bi, i, 0, 0)),
        ],
        out_shape=[
            jax.ShapeDtypeStruct((b, n, D_ROW), F32),
            jax.ShapeDtypeStruct((b, nt, 8, t_tok), I32),
        ],
        compiler_params=pltpu.CompilerParams(
            dimension_semantics=("arbitrary", "arbitrary"), vmem_limit_bytes=VMEM_LIMIT),
        name="outproj_router",
    )(x, ma, ms, wa, ws, fg, wr)


def _moe_kernel(rows_ref, nval_ref, e1_ref, e2_ref, nused_ref,
                h_hbm, wgu1_ref, wd1_ref, wgu2_ref, wd2_ref, fg_ref, og_ref, out_hbm,
                xbuf, obuf, gsem, ssem):
    t = pl.program_id(0)
    n_used = nused_ref[0]
    slot = t & 1

    def row_in(tile, s):
        def make(r):
            tok = rows_ref[tile * T_MOE + r]
            return pltpu.make_async_copy(h_hbm.at[pl.ds(tok, 1), :], xbuf.at[s, pl.ds(r, 1), :], gsem.at[s])
        return make

    def row_out(tile, s):
        def make(r):
            tok = rows_ref[tile * T_MOE + r]
            return pltpu.make_async_copy(obuf.at[s, pl.ds(r, 1), :], out_hbm.at[pl.ds(tok, 1), :], ssem.at[s])
        return make

    def all_in(s):
        return pltpu.make_async_copy(h_hbm.at[pl.ds(0, T_MOE), :], xbuf.at[s], gsem.at[s])

    def all_out(s):
        return pltpu.make_async_copy(obuf.at[s], out_hbm.at[pl.ds(0, T_MOE), :], ssem.at[s])

    def for_rows(n, fn):
        def body(r, c):
            fn(r)
            return c
        lax.fori_loop(0, n, body, 0)

    def wait_rows(tile, make, whole):
        nv = nval_ref[tile]

        @pl.when(nv == T_MOE)
        def _():
            whole.wait()

        @pl.when(nv < T_MOE)
        def _():
            for_rows(nv, lambda r: make(r).wait())

    def start_rows(tile, make):
        nv = nval_ref[tile]

        @pl.when(nv == T_MOE)
        def _():
            for r in range(T_MOE):
                make(r).start()

        @pl.when(nv < T_MOE)
        def _():
            for_rows(nv, lambda r: make(r).start())

    @pl.when(t == 0)
    def _():
        xbuf[...] = jnp.zeros(xbuf.shape, F32)
        for_rows(nval_ref[0], lambda r: row_in(0, 0)(r).start())

    @pl.when(t + 1 < n_used)
    def _():
        start_rows(t + 1, row_in(t + 1, 1 - slot))

    @pl.when(t < n_used)
    def _():
        wait_rows(t, row_in(t, slot), all_in(slot))

        @pl.when(t >= 2)
        def _():
            wait_rows(t - 2, row_out(t - 2, slot), all_out(slot))

        xw = xbuf[slot]
        x = xw[:, :D_MODEL]
        w = xw[:, D_MODEL:]
        hn = _rms(x, fg_ref[...]).astype(BF16)
        y = jnp.zeros(x.shape, F32)
        for j, (wgu_ref, wd_ref) in enumerate(((wgu1_ref, wd1_ref), (wgu2_ref, wd2_ref))):
            gu = jnp.dot(hn, wgu_ref[0], preferred_element_type=F32)
            g = gu[:, :D_EXPERT]
            hd = g / (1.0 + jnp.exp(-g)) * gu[:, D_EXPERT:] * w[:, j:j + 1]
            y = y + jnp.dot(hd.astype(BF16), wd_ref[0], preferred_element_type=F32)
        obuf[slot] = _rms(x + y, og_ref[...])
        start_rows(t, row_out(t, slot))

        @pl.when(t == n_used - 1)
        def _():
            @pl.when(t >= 1)
            def _():
                wait_rows(t - 1, row_out(t - 1, 1 - slot), all_out(1 - slot))
            wait_rows(t, row_out(t, slot), all_out(slot))


def _moe_call(rows, nval, e1, e2, nused, h1, wgu, wd, fg, og):
    n_tok = h1.shape[0]
    d = D_MODEL
    n_tiles = rows.shape[0] // T_MOE
    grid_spec = pltpu.PrefetchScalarGridSpec(
        num_scalar_prefetch=5,
        grid=(n_tiles,),
        in_specs=[
            pl.BlockSpec(memory_space=pl.ANY),
            pl.BlockSpec((1, d, 2 * D_EXPERT), lambda t, rows, nval, e1, e2, nu: (e1[t], 0, 0)),
            pl.BlockSpec((1, D_EXPERT, d), lambda t, rows, nval, e1, e2, nu: (e1[t], 0, 0)),
            pl.BlockSpec((1, d, 2 * D_EXPERT), lambda t, rows, nval, e1, e2, nu: (e2[t], 0, 0)),
            pl.BlockSpec((1, D_EXPERT, d), lambda t, rows, nval, e1, e2, nu: (e2[t], 0, 0)),
            pl.BlockSpec((1, d), lambda t, *_: (0, 0)),
            pl.BlockSpec((1, d), lambda t, *_: (0, 0)),
        ],
        out_specs=pl.BlockSpec(memory_space=pl.ANY),
        scratch_shapes=[
            pltpu.VMEM((2, T_MOE, D_ROW), F32),
            pltpu.VMEM((2, T_MOE, d), F32),
            pltpu.SemaphoreType.DMA((2,)),
            pltpu.SemaphoreType.DMA((2,)),
        ],
    )
    return pl.pallas_call(
        _moe_kernel,
        grid_spec=grid_spec,
        out_shape=jax.ShapeDtypeStruct((n_tok, d), F32),
        compiler_params=pltpu.CompilerParams(
            dimension_semantics=("arbitrary",), vmem_limit_bytes=VMEM_LIMIT),
        name="moe",
    )(rows, nval, e1, e2, nused, h1, wgu, wd, wgu, wd, fg, og)


def _rel_bucket(rel):
    half = N_BUCKETS // 2
    max_exact = half // 2
    ret = jnp.where(rel > 0, half, 0)
    n = jnp.abs(rel)
    nf = jnp.maximum(n, 1).astype(F32)
    large = max_exact + (jnp.log(nf / max_exact) / math.log(MAX_DIST / max_exact)
                         * (half - max_exact)).astype(I32)
    large = jnp.minimum(large, half - 1)
    return ret + jnp.where(n < max_exact, n, large)


def _bias_tables(rel_table):
    table = rel_table.astype(F32)
    far = table[_rel_bucket(jnp.array(-8 * MAX_DIST))]
    n_rel = 3 * TQ - 1
    rel = jnp.arange(-(2 * TQ - 1), TQ)
    rev = (table[_rel_bucket(rel)] - far)[::-1].T

    def rows(start, n_rows):
        rolled = jnp.roll(rev, -start, axis=1)
        flat = jnp.tile(rolled, (1, n_rows + 1))[:, :n_rows * (n_rel - 1)]
        return flat.reshape(-1, n_rows, n_rel - 1)[:, :, :TQ]

    prev = rows(n_rel - TQ, TQ)
    diag = rows(TQ - 1, TQ)
    meta0 = rows(TQ + N_META - 1, N_META)
    kk = jnp.arange(TQ)[:, None]
    qq = jnp.arange(TQ)[None, :]
    diag = jnp.where(kk // CHUNK <= qq // CHUNK, diag, NEG_INF)
    masked = jnp.full_like(diag, NEG_INF)
    near = jnp.stack([jnp.concatenate([diag, masked], axis=1),
                      jnp.concatenate([prev, diag], axis=1)], axis=1)
    meta = jnp.stack([meta0, jnp.zeros_like(meta0)], axis=1)
    return near * LOG2E, meta * LOG2E


def _route_tables(cls, n_tiles):
    n = cls.shape[0]
    order = jnp.argsort(cls, stable=True).astype(I32)
    counts = jnp.sum(cls[:, None] == jnp.arange(N_CLASSES)[None, :], axis=0).astype(I32)
    padded = (counts + T_MOE - 1) // T_MOE * T_MOE
    pend = jnp.cumsum(padded)
    pstart = pend - padded
    ustart = jnp.cumsum(counts) - counts
    n_used = pend[-1] // T_MOE
    tile = jnp.arange(n_tiles, dtype=I32)
    tcls = jnp.sum(pend[None, :] <= (tile * T_MOE)[:, None], axis=1).astype(I32)
    tcls = jnp.minimum(tcls, N_CLASSES - 1)
    last = tcls[jnp.maximum(n_used - 1, 0)]
    used = tile < n_used
    tcls = jnp.where(used, tcls, last)
    nval = jnp.where(used, jnp.clip(pstart[tcls] + counts[tcls] - tile * T_MOE, 0, T_MOE), 0).astype(I32)
    r = jnp.arange(T_MOE, dtype=I32)
    src = (ustart[tcls] + tile * T_MOE - pstart[tcls])[:, None] + r[None, :]
    valid = r[None, :] < nval[:, None]
    rows = jnp.where(valid, order[jnp.clip(src, 0, n - 1)], 0).reshape(-1)
    grp = tcls // len(PAIR_LO)
    pair = tcls % len(PAIR_LO)
    e1 = grp * EXPERTS_PER_GROUP + jnp.array(PAIR_LO, I32)[pair]
    e2 = grp * EXPERTS_PER_GROUP + jnp.array(PAIR_HI, I32)[pair]
    return rows, nval, e1.astype(I32), e2.astype(I32), n_used.astype(I32).reshape(1)


def kernel(x, meta_tokens, rel_table, attn_norm, w_in, lambda_q1, lambda_k1, lambda_q2, lambda_k2,
           diff_norm, sb_norm, w_out, ffn_norm, w_group, w_router, w_gate, w_up, w_down, final_norm):
    b, n, d = x.shape
    assert d == D_MODEL and attn_norm.shape[0] == 1
    t_tok = min(T_PROJ, n)
    assert n % t_tok == 0 and n % TQ == 0 and (b * n) % T_MOE == 0

    w = w_in[0]
    scale = HEAD_DIM ** -0.5
    dq, dk, dv, sq, sk, sv = [w[:, i * 512:(i + 1) * 512] for i in range(6)]
    wk = jnp.concatenate([dk, sk], axis=1).astype(BF16)
    wqvt = jnp.concatenate([dq * (scale * LOG2E), dv, sq * scale, sv], axis=1).T.astype(BF16)
    g_attn = attn_norm[0].reshape(1, d)

    kk, qvt = _proj_call(x, g_attn, wk, wqvt, t_tok, TQ)
    meta_pad = jnp.zeros((1, 128, d), x.dtype).at[0, :N_META].set(meta_tokens.astype(x.dtype))
    kk_m, qvt_m = _proj_call(meta_pad, g_attn, wk, wqvt, 128, 128)
    k_meta = kk_m[0, :N_META]
    vt_meta = qvt_m[0, 0, :, :N_META]

    bias_near, bias_meta = _bias_tables(rel_table)
    lamp = jnp.stack([lambda_q1[0], lambda_k1[0], lambda_q2[0], lambda_k2[0]]).astype(F32)
    mixed_a = _diff_call(lamp, kk, qvt, k_meta, vt_meta, bias_near, bias_meta,
                         diff_norm[0].astype(F32).reshape(DIFF_V_DIM, 1))
    mixed_s = _sb_call(kk, qvt, k_meta, vt_meta, sb_norm[0].astype(F32).reshape(HEAD_DIM, 1))

    wo = w_out[0].astype(BF16)
    wr = jnp.concatenate([w_group[0], w_router[0], jnp.zeros((d, 12), F32)], axis=1).T
    wr_hi = wr.astype(BF16)
    wr = jnp.concatenate([wr_hi, (wr - wr_hi.astype(F32)).astype(BF16)], axis=0)
    h1, rc = _outproj_call(x, mixed_a, mixed_s, wo[:512], wo[512:], ffn_norm[0].reshape(1, d), wr, t_tok)

    n_tok = b * n
    n_tiles = n_tok // T_MOE + N_CLASSES
    rows, nval, e1, e2, n_used = _route_tables(rc[:, :, 0, :].reshape(n_tok), n_tiles)

    wgu = jnp.concatenate([w_gate[0], w_up[0]], axis=2).astype(BF16)
    wd = w_down[0].astype(BF16)
    out = _moe_call(rows, nval, e1, e2, n_used, h1.reshape(n_tok, D_ROW), wgu, wd,
                    ffn_norm[0].reshape(1, d), final_norm.reshape(1, d))
    return out.reshape(b, n, d)
```

```python
import functools
import math

import jax
import jax.numpy as jnp
from jax import lax
from jax.experimental import pallas as pl
from jax.experimental.pallas import tpu as pltpu

F32 = jnp.float32
BF16 = jnp.bfloat16
I32 = jnp.int32

D_MODEL = 1024
CHUNK = 64
N_META = 16
HEAD_DIM = 64
N_DIFF_HEADS = 4
DIFF_V_DIM = 128
N_SB_HEADS = 8
N_BUCKETS = 32
MAX_DIST = 128
N_GROUPS = 4
EXPERTS_PER_GROUP = 4
N_EXPERTS = 16
D_EXPERT = 512
EPS = 1e-6
NEG_INF = -1e30
LAMBDA_INIT = 0.8 - 0.6 * math.exp(-0.3 * 0)
LOG2E = math.log2(math.e)

PAIR_LO = (0, 0, 0, 1, 1, 2)
PAIR_HI = (1, 2, 3, 2, 3, 3)
N_CLASSES = N_GROUPS * len(PAIR_LO)

TQ = 256
SB_GROUP = 8
DIFF_GROUP = 2
SB_UNDERFLOW = 110.0
T_PROJ = 512
T_MOE = 256
D_ROW = D_MODEL + 128
VMEM_LIMIT = 48 * 1024 * 1024


def _rms(x, g):
    return x * lax.rsqrt(jnp.mean(x * x, axis=-1, keepdims=True) + EPS) * g


def _proj_kernel(x_ref, g_ref, wk_ref, wqvt_ref, k_ref, qvt_ref, *, n_sub, t_sub):
    hn = _rms(x_ref[0], g_ref[...]).astype(BF16)
    k_ref[0] = jnp.dot(hn, wk_ref[...], preferred_element_type=F32).astype(BF16)
    n_rows = wqvt_ref.shape[0]
    for r in range(0, n_rows, 512):
        blk = lax.dot_general(wqvt_ref[r:r + 512, :], hn, (((1,), (1,)), ((), ())),
                              preferred_element_type=F32)
        for j in range(n_sub):
            qvt_ref[0, j, r:r + 512, :] = blk[:, j * t_sub:(j + 1) * t_sub].astype(BF16)


def _proj_call(x, g, wk, wqvt, t_tok, t_sub):
    b, n, d = x.shape
    n_sub = t_tok // t_sub
    kern = functools.partial(_proj_kernel, n_sub=n_sub, t_sub=t_sub)
    return pl.pallas_call(
        kern,
        grid=(b, n // t_tok),
        in_specs=[
            pl.BlockSpec((1, t_tok, d), lambda bi, i: (bi, i, 0)),
            pl.BlockSpec((1, d), lambda bi, i: (0, 0)),
            pl.BlockSpec(wk.shape, lambda bi, i: (0, 0)),
            pl.BlockSpec(wqvt.shape, lambda bi, i: (0, 0)),
        ],
        out_specs=[
            pl.BlockSpec((1, t_tok, wk.shape[1]), lambda bi, i: (bi, i, 0)),
            pl.BlockSpec((1, n_sub, wqvt.shape[0], t_sub), lambda bi, i: (bi, i, 0, 0)),
        ],
        out_shape=[
            jax.ShapeDtypeStruct((b, n, wk.shape[1]), BF16),
            jax.ShapeDtypeStruct((b, n // t_sub, wqvt.shape[0], t_sub), BF16),
        ],
        compiler_params=pltpu.CompilerParams(
            dimension_semantics=("arbitrary", "arbitrary"), vmem_limit_bytes=VMEM_LIMIT),
        name="proj",
    )(x, g, wk, wqvt)


def _diff_kernel(lam_ref, q_ref, k_ref, v_ref, km_ref, vm_ref, bias_ref, bm_ref, dn_ref, o_ref,
                 m_s, l_s, a_s, s_buf, mx_buf, p_buf, alpha_buf):
    qi = pl.program_id(2)
    q = q_ref[0, 0]
    zq = jnp.zeros((HEAD_DIM, TQ), BF16)

    chains = []
    for h in range(DIFF_GROUP):
        blk = slice(h * 128, (h + 1) * 128)
        q1, q2 = q[h * 128:h * 128 + HEAD_DIM], q[h * 128 + HEAD_DIM:(h + 1) * 128]
        chains.append((2 * h, h, blk, jnp.concatenate([q1, zq], axis=0)))
        chains.append((2 * h + 1, h, blk, jnp.concatenate([zq, q2], axis=0)))

    m_s[...] = jnp.full(m_s.shape, NEG_INF, F32)
    l_s[...] = jnp.zeros(l_s.shape, F32)
    a_s[...] = jnp.zeros(a_s.shape, F32)

    def step(parts):
        scores = [[jnp.dot(kblk[:, blk], qq, preferred_element_type=F32) + bias[h, 0]
                   for kblk, _, bias in parts] for _, h, blk, qq in chains]
        m_new = []
        for (c, _, _, _), sc in zip(chains, scores):
            m = m_s[c]
            for s in sc:
                m = jnp.maximum(m, jnp.max(s, axis=0, keepdims=True))
            m_new.append(m)
        probs = [[jnp.exp2(s - m) for s in sc] for sc, m in zip(scores, m_new)]
        for (c, _, blk, _), m, pr in zip(chains, m_new, probs):
            alpha = jnp.exp2(m_s[c] - m)
            l_new = alpha * l_s[c]
            acc = alpha * a_s[c]
            for (_, vts, _), p in zip(parts, pr):
                l_new = l_new + jnp.sum(p, axis=0, keepdims=True)
                pb = p.astype(BF16)
                off = 0
                for vt in vts:
                    w = vt.shape[1]
                    acc = acc + jnp.dot(vt[blk, :], pb[off:off + w], preferred_element_type=F32)
                    off += w
            l_s[c] = l_new
            a_s[c] = acc
            m_s[c] = m

    def k_rows(tile, n_tiles):
        return k_ref[0, pl.ds(pl.multiple_of(tile * TQ, TQ), n_tiles * TQ), :]

    t0 = jnp.maximum(qi - 1, 0)
    step([(km_ref[...], [vm_ref[...]], bm_ref),
          (k_rows(t0, 2), [v_ref[0, t0], v_ref[0, t0 + 1]], bias_ref)])

    def scores_into(t, buf):
        kblk = k_rows(t, 1)
        for c, _, blk, qq in chains:
            s = jnp.dot(kblk[:, blk], qq, preferred_element_type=F32)
            s_buf[buf, c] = s
            mx_buf[buf, c] = jnp.max(s, axis=0, keepdims=True)

    def softmax_update(buf, live=None):
        m_old = [m_s[c] for c, _, _, _ in chains]
        m_new = [jnp.maximum(m, mx_buf[buf, c]) for c, m in enumerate(m_old)]
        p = [jnp.exp2(s_buf[buf, c] - m) for c, m in enumerate(m_new)]
        if live is not None:
            m_new = [jnp.where(live, mn, mo) for mn, mo in zip(m_new, m_old)]
            p = [jnp.where(live, x, 0.0) for x in p]
        alpha = [jnp.exp2(mo - mn) for mo, mn in zip(m_old, m_new)]
        for c, (al, x, mn) in enumerate(zip(alpha, p, m_new)):
            l_s[c] = al * l_s[c] + jnp.sum(x, axis=0, keepdims=True)
            m_s[c] = mn
        return [(al, x.astype(BF16)) for al, x in zip(alpha, p)]

    def update_from(t, buf, live=None):
        vt = v_ref[0, t]
        for (c, _, blk, _), (alpha, pb) in zip(chains, softmax_update(buf, live)):
            a_s[c] = alpha * a_s[c] + jnp.dot(vt[blk, :], pb, preferred_element_type=F32)

    def update_deferred(buf):
        for c, (alpha, pb) in enumerate(softmax_update(buf)):
            alpha_buf[c], p_buf[c] = alpha, pb

    def finish_deferred(t):
        vt = v_ref[0, t]
        for c, _, blk, _ in chains:
            a_s[c] = alpha_buf[c] * a_s[c] + jnp.dot(vt[blk, :], p_buf[c], preferred_element_type=F32)

    alpha_buf[...] = jnp.ones(alpha_buf.shape, F32)
    p_buf[...] = jnp.zeros(p_buf.shape, BF16)
    scores_into(0, 0)

    def far_pair(t):
        finish_deferred(jnp.maximum(t - 1, 0))
        scores_into(t + 1, 1)
        update_from(t, 0)
        scores_into(t + 2, 0)
        update_deferred(1)

    def far_quad(i, carry):
        far_pair(4 * i)
        far_pair(4 * i + 2)
        return carry

    def far_rest(t, carry):
        far_pair(2 * t)
        return carry

    n_pairs = t0 // 2
    lax.fori_loop(0, n_pairs // 2, far_quad, 0)
    lax.fori_loop(n_pairs // 2 * 2, n_pairs, far_rest, 0)
    finish_deferred(jnp.maximum(2 * n_pairs - 1, 0))
    update_from(jnp.maximum(t0 - 1, 0), 0, live=(t0 & 1) == 1)

    lp = lam_ref[...]
    lam = (jnp.exp(jnp.sum(lp[0:1] * lp[1:2], axis=-1, keepdims=True))
           - jnp.exp(jnp.sum(lp[2:3] * lp[3:4], axis=-1, keepdims=True)) + LAMBDA_INIT)
    ys = []
    for h in range(DIFF_GROUP):
        o = a_s[2 * h] / l_s[2 * h] - lam * (a_s[2 * h + 1] / l_s[2 * h + 1])
        ys.append(o * lax.rsqrt(jnp.mean(o * o, axis=0, keepdims=True) + EPS) * dn_ref[...])
    o_ref[0] = (jnp.concatenate(ys, axis=0) * (1.0 - LAMBDA_INIT)).T.astype(BF16)


def _diff_call(lamp, kk, qvt, k_meta, vt_meta, bias_near, bias_meta, dn_col):
    b, n, _ = kk.shape
    nt = n // TQ
    g = DIFF_GROUP
    w = g * 128
    n_chains = 2 * g
    v_off = N_DIFF_HEADS // g
    return pl.pallas_call(
        _diff_kernel,
        grid=(b, N_DIFF_HEADS // g, nt),
        in_specs=[
            pl.BlockSpec(lamp.shape, lambda bi, h, qi: (0, 0)),
            pl.BlockSpec((1, 1, w, TQ), lambda bi, h, qi: (bi, qi, h, 0)),
            pl.BlockSpec((1, n, w), lambda bi, h, qi: (bi, 0, h)),
            pl.BlockSpec((1, nt, w, TQ), lambda bi, h, qi: (bi, 0, v_off + h, 0)),
            pl.BlockSpec((N_META, w), lambda bi, h, qi: (0, h)),
            pl.BlockSpec((w, N_META), lambda bi, h, qi: (v_off + h, 0)),
            pl.BlockSpec((g, 1, 2 * TQ, TQ), lambda bi, h, qi: (h, jnp.minimum(qi, 1), 0, 0)),
            pl.BlockSpec((g, 1, N_META, TQ), lambda bi, h, qi: (h, jnp.minimum(qi, 1), 0, 0)),
            pl.BlockSpec((DIFF_V_DIM, 1), lambda bi, h, qi: (0, 0)),
        ],
        out_specs=pl.BlockSpec((1, TQ, w), lambda bi, h, qi: (bi, qi, h)),
        out_shape=jax.ShapeDtypeStruct((b, n, N_DIFF_HEADS * DIFF_V_DIM), BF16),
        scratch_shapes=[
            pltpu.VMEM((n_chains, 1, TQ), F32), pltpu.VMEM((n_chains, 1, TQ), F32),
            pltpu.VMEM((n_chains, DIFF_V_DIM, TQ), F32),
            pltpu.VMEM((2, n_chains, TQ, TQ), F32), pltpu.VMEM((2, n_chains, 1, TQ), F32),
            pltpu.VMEM((n_chains, TQ, TQ), BF16), pltpu.VMEM((n_chains, 1, TQ), F32),
        ],
        compiler_params=pltpu.CompilerParams(
            dimension_semantics=("arbitrary", "arbitrary", "arbitrary"), vmem_limit_bytes=VMEM_LIMIT),
        name="diff_attn",
    )(lamp, qvt, kk, qvt, k_meta, vt_meta, bias_near, bias_meta, dn_col)


def _softplus(z):
    return jnp.maximum(z, 0.0) + jnp.log(1.0 + jnp.exp(-jnp.abs(z)))


def _sb_kernel(q_ref, k_ref, v_ref, km_ref, vm_ref, g_ref, o_ref, r_ref, a_ref):
    qi = pl.program_id(2)
    q = q_ref[0, 0]
    zq = jnp.zeros((HEAD_DIM, TQ), BF16)

    def q_padded(g):
        qg = q[g * HEAD_DIM:(g + 1) * HEAD_DIM]
        return jnp.concatenate([qg, zq] if g % 2 == 0 else [zq, qg], axis=0)

    qs = [q_padded(g) for g in range(SB_GROUP)]
    heads = [(g, slice(g * HEAD_DIM, (g + 1) * HEAD_DIM), slice((g // 2) * 128, (g // 2 + 1) * 128))
             for g in range(SB_GROUP)]

    def tri(n):
        return (lax.broadcasted_iota(I32, (n, n), 0) <= lax.broadcasted_iota(I32, (n, n), 1)).astype(BF16)

    def suffix_sums(t, sp):
        hi = sp.astype(BF16)
        lo = (sp - hi.astype(F32)).astype(BF16)
        return jnp.dot(t, hi, preferred_element_type=F32) + jnp.dot(t, lo, preferred_element_type=F32)

    def k_rows(tile, n_tiles):
        return k_ref[0, pl.ds(pl.multiple_of(tile * TQ, TQ), n_tiles * TQ), :]

    t0 = jnp.maximum(qi - 1, 0)
    has_prev = qi > 0
    k_prev, k_diag = k_rows(t0, 1), k_rows(qi, 1)
    v_prev, v_diag = v_ref[0, t0], v_ref[0, qi]
    keep = lax.broadcasted_iota(I32, (TQ, TQ), 0) < lax.broadcasted_iota(I32, (TQ, TQ), 1)
    t_full = tri(TQ)
    z_diag = [jnp.where(keep, jnp.dot(k_diag[:, lanes], qs[g], preferred_element_type=F32), NEG_INF)
              for g, _, lanes in heads]
    z_prev = [jnp.dot(k_prev[:, lanes], qs[g], preferred_element_type=F32) for g, _, lanes in heads]
    suf_diag = [suffix_sums(t_full, _softplus(z)) for z in z_diag]
    suf_prev = [suffix_sums(t_full, _softplus(z)) for z in z_prev]
    a_diag = [jnp.exp(z - s).astype(BF16) for z, s in zip(z_diag, suf_diag)]
    a_prev = [jnp.exp(z - s - sd[0:1, :]).astype(BF16) for z, s, sd in zip(z_prev, suf_prev, suf_diag)]
    for g, rows, _ in heads:
        acc_prev = jnp.dot(v_prev[rows, :], a_prev[g], preferred_element_type=F32)
        a_ref[rows, :] = (jnp.dot(v_diag[rows, :], a_diag[g], preferred_element_type=F32)
                          + jnp.where(has_prev, acc_prev, 0.0))
        r_ref[g:g + 1, :] = suf_diag[g][0:1, :] + jnp.where(has_prev, suf_prev[g][0:1, :], 0.0)

    def step(kblk, vt):
        t = tri(kblk.shape[0])
        for g, rows, lanes in heads:
            z = jnp.dot(kblk[:, lanes], qs[g], preferred_element_type=F32)
            suf = suffix_sums(t, _softplus(z))
            r_old = r_ref[g:g + 1, :]
            a = jnp.exp(z - suf - r_old)
            a_ref[rows, :] += jnp.dot(vt[rows, :], a.astype(BF16), preferred_element_type=F32)
            r_ref[g:g + 1, :] = r_old + suf[0:1, :]

    def more(c):
        i, r_min = c
        return jnp.logical_and(i < t0, r_min < SB_UNDERFLOW)

    def left(c):
        i, _ = c
        kt = t0 - 1 - i
        step(k_rows(kt, 1), v_ref[0, kt])
        return i + 1, jnp.min(r_ref[...])

    _, r_min = lax.while_loop(more, left, (jnp.int32(0), jnp.min(r_ref[...])))

    @pl.when(r_min < SB_UNDERFLOW)
    def _():
        step(km_ref[...], vm_ref[...])

    ys = []
    for g, rows, _ in heads:
        o = a_ref[rows, :]
        ys.append(o * lax.rsqrt(jnp.mean(o * o, axis=0, keepdims=True) + EPS) * g_ref[...])
    o_ref[0] = jnp.concatenate(ys, axis=0).T.astype(BF16)


def _sb_call(kk, qvt, k_meta, vt_meta, g_col):
    b, n, _ = kk.shape
    nt = n // TQ
    w = SB_GROUP * HEAD_DIM
    n_groups = N_SB_HEADS // SB_GROUP
    k_off = N_DIFF_HEADS * 128 // w
    q_off = 2 * N_DIFF_HEADS * 128 // w
    v_off = 3 * N_DIFF_HEADS * 128 // w
    return pl.pallas_call(
        _sb_kernel,
        grid=(b, n_groups, nt),
        in_specs=[
            pl.BlockSpec((1, 1, w, TQ), lambda bi, h, qi: (bi, qi, q_off + h, 0)),
            pl.BlockSpec((1, n, w), lambda bi, h, qi: (bi, 0, k_off + h)),
            pl.BlockSpec((1, nt, w, TQ), lambda bi, h, qi: (bi, 0, v_off + h, 0)),
            pl.BlockSpec((N_META, w), lambda bi, h, qi: (0, k_off + h)),
            pl.BlockSpec((w, N_META), lambda bi, h, qi: (v_off + h, 0)),
            pl.BlockSpec((HEAD_DIM, 1), lambda bi, h, qi: (0, 0)),
        ],
        out_specs=pl.BlockSpec((1, TQ, w), lambda bi, h, qi: (bi, qi, h)),
        out_shape=jax.ShapeDtypeStruct((b, n, N_SB_HEADS * HEAD_DIM), BF16),
        scratch_shapes=[pltpu.VMEM((SB_GROUP, TQ), F32), pltpu.VMEM((w, TQ), F32)],
        compiler_params=pltpu.CompilerParams(
            dimension_semantics=("arbitrary", "arbitrary", "arbitrary"), vmem_limit_bytes=VMEM_LIMIT),
        name="sb_attn",
    )(qvt, kk, qvt, k_meta, vt_meta, g_col)


def _outproj_kernel(x_ref, ma_ref, ms_ref, wa_ref, ws_ref, fg_ref, wr_ref, h_ref, rc_ref):
    h1 = (x_ref[0] + jnp.dot(ma_ref[0], wa_ref[...], preferred_element_type=F32)
          + jnp.dot(ms_ref[0], ws_ref[...], preferred_element_type=F32))
    h_ref[0, :, :D_MODEL] = h1
    hn = _rms(h1, fg_ref[...])
    hn_hi = hn.astype(BF16)
    hn_lo = (hn - hn_hi.astype(F32)).astype(BF16)
    nt_dims = (((1,), (1,)), ((), ()))
    by_hi = lax.dot_general(wr_ref[...], hn_hi, nt_dims, preferred_element_type=F32)
    lg = (by_hi[:32] + by_hi[32:]
          + lax.dot_general(wr_ref[:32, :], hn_lo, nt_dims, preferred_element_type=F32))
    t = lg.shape[1]
    rows = [lg[i:i + 1, :] for i in range(N_GROUPS + N_EXPERTS)]

    def first_argmax(vals, skip=None):
        best, idx = None, None
        for j, v in enumerate(vals):
            if skip is not None:
                v = jnp.where(skip == j, -jnp.inf, v)
            if best is None:
                best, idx = v, jnp.zeros((1, t), I32)
            else:
                better = v > best
                idx = jnp.where(better, j, idx)
                best = jnp.where(better, v, best)
        return best, idx

    g_best, g_idx = first_argmax(rows[:N_GROUPS])
    g_w = 1.0 / sum(jnp.exp(r - g_best) for r in rows[:N_GROUPS])
    el = []
    for j in range(EXPERTS_PER_GROUP):
        v = rows[N_GROUPS + j]
        for g in range(1, N_GROUPS):
            v = jnp.where(g_idx == g, rows[N_GROUPS + EXPERTS_PER_GROUP * g + j], v)
        el.append(v)
    e1, i1 = first_argmax(el)
    e2, i2 = first_argmax(el, skip=i1)
    p2 = jnp.exp(e2 - e1)
    w1 = g_w / (1.0 + p2)
    w2 = g_w * p2 / (1.0 + p2)
    lo = jnp.minimum(i1, i2)
    hi = jnp.maximum(i1, i2)
    pair = jnp.where(lo == 0, 0, jnp.where(lo == 1, 3, 5)) + (hi - lo - 1)
    cls = g_idx * len(PAIR_LO) + pair
    first_is_lo = i1 < i2
    w_lo = jnp.where(first_is_lo, w1, w2)
    w_hi = jnp.where(first_is_lo, w2, w1)
    h_ref[0, :, D_MODEL:] = jnp.concatenate([w_lo, w_hi, jnp.zeros((126, t), F32)], axis=0).T
    rc_ref[0, 0] = jnp.concatenate([cls, jnp.zeros((7, t), I32)], axis=0)


def _outproj_call(x, ma, ms, wa, ws, fg, wr, t_tok):
    b, n, d = x.shape
    nt = n // t_tok
    return pl.pallas_call(
        _outproj_kernel,
        grid=(b, nt),
        in_specs=[
            pl.BlockSpec((1, t_tok, d), lambda bi, i: (bi, i, 0)),
            pl.BlockSpec((1, t_tok, ma.shape[2]), lambda bi, i: (bi, i, 0)),
            pl.BlockSpec((1, t_tok, ms.shape[2]), lambda bi, i: (bi, i, 0)),
            pl.BlockSpec(wa.shape, lambda bi, i: (0, 0)),
            pl.BlockSpec(ws.shape, lambda bi, i: (0, 0)),
            pl.BlockSpec((1, d), lambda bi, i: (0, 0)),
            pl.BlockSpec(wr.shape, lambda bi, i: (0, 0)),
        ],
        out_specs=[
            pl.BlockSpec((1, t_tok, D_ROW), lambda bi, i: (bi, i, 0)),
            pl.BlockSpec((1, 1, 8, t_tok), lambda bi, i: (bi, i, 0, 0)),
        ],
        out_shape=[
            jax.ShapeDtypeStruct((b, n, D_ROW), F32),
            jax.ShapeDtypeStruct((b, nt, 8, t_tok), I32),
        ],
        compiler_params=pltpu.CompilerParams(
            dimension_semantics=("arbitrary", "arbitrary"), vmem_limit_bytes=VMEM_LIMIT),
        name="outproj_router",
    )(x, ma, ms, wa, ws, fg, wr)


def _moe_kernel(rows_ref, nval_ref, e1_ref, e2_ref, nused_ref,
                h_hbm, wgu1_ref, wd1_ref, wgu2_ref, wd2_ref, fg_ref, og_ref, out_hbm,
                xbuf, obuf, gsem, ssem):
    t = pl.program_id(0)
    n_used = nused_ref[0]
    slot = t & 1

    def row_in(tile, s):
        def make(r):
            tok = rows_ref[tile * T_MOE + r]
            return pltpu.make_async_copy(h_hbm.at[pl.ds(tok, 1), :], xbuf.at[s, pl.ds(r, 1), :], gsem.at[s])
        return make

    def row_out(tile, s):
        def make(r):
            tok = rows_ref[tile * T_MOE + r]
            return pltpu.make_async_copy(obuf.at[s, pl.ds(r, 1), :], out_hbm.at[pl.ds(tok, 1), :], ssem.at[s])
        return make

    def all_in(s):
        return pltpu.make_async_copy(h_hbm.at[pl.ds(0, T_MOE), :], xbuf.at[s], gsem.at[s])

    def all_out(s):
        return pltpu.make_async_copy(obuf.at[s], out_hbm.at[pl.ds(0, T_MOE), :], ssem.at[s])

    def for_rows(n, fn):
        def body(r, c):
            fn(r)
            return c
        lax.fori_loop(0, n, body, 0)

    def wait_rows(tile, make, whole):
        nv = nval_ref[tile]

        @pl.when(nv == T_MOE)
        def _():
            whole.wait()

        @pl.when(nv < T_MOE)
        def _():
            for_rows(nv, lambda r: make(r).wait())

    def start_rows(tile, make):
        nv = nval_ref[tile]

        @pl.when(nv == T_MOE)
        def _():
            for r in range(T_MOE):
                make(r).start()

        @pl.when(nv < T_MOE)
        def _():
            for_rows(nv, lambda r: make(r).start())

    @pl.when(t == 0)
    def _():
        xbuf[...] = jnp.zeros(xbuf.shape, F32)
        for_rows(nval_ref[0], lambda r: row_in(0, 0)(r).start())

    @pl.when(t + 1 < n_used)
    def _():
        start_rows(t + 1, row_in(t + 1, 1 - slot))

    @pl.when(t < n_used)
    def _():
        wait_rows(t, row_in(t, slot), all_in(slot))

        @pl.when(t >= 2)
        def _():
            wait_rows(t - 2, row_out(t - 2, slot), all_out(slot))

        xw = xbuf[slot]
        x = xw[:, :D_MODEL]
        w = xw[:, D_MODEL:]
        hn = _rms(x, fg_ref[...]).astype(BF16)
        y = jnp.zeros(x.shape, F32)
        for j, (wgu_ref, wd_ref) in enumerate(((wgu1_ref, wd1_ref), (wgu2_ref, wd2_ref))):
            gu = jnp.dot(hn, wgu_ref[0], preferred_element_type=F32)
            g = gu[:, :D_EXPERT]
            hd = g / (1.0 + jnp.exp(-g)) * gu[:, D_EXPERT:] * w[:, j:j + 1]
            y = y + jnp.dot(hd.astype(BF16), wd_ref[0], preferred_element_type=F32)
        obuf[slot] = _rms(x + y, og_ref[...])
        start_rows(t, row_out(t, slot))

        @pl.when(t == n_used - 1)
        def _():
            @pl.when(t >= 1)
            def _():
                wait_rows(t - 1, row_out(t - 1, 1 - slot), all_out(1 - slot))
            wait_rows(t, row_out(t, slot), all_out(slot))


def _moe_call(rows, nval, e1, e2, nused, h1, wgu, wd, fg, og):
    n_tok = h1.shape[0]
    d = D_MODEL
    n_tiles = rows.shape[0] // T_MOE
    grid_spec = pltpu.PrefetchScalarGridSpec(
        num_scalar_prefetch=5,
        grid=(n_tiles,),
        in_specs=[
            pl.BlockSpec(memory_space=pl.ANY),
            pl.BlockSpec((1, d, 2 * D_EXPERT), lambda t, rows, nval, e1, e2, nu: (e1[t], 0, 0)),
            pl.BlockSpec((1, D_EXPERT, d), lambda t, rows, nval, e1, e2, nu: (e1[t], 0, 0)),
            pl.BlockSpec((1, d, 2 * D_EXPERT), lambda t, rows, nval, e1, e2, nu: (e2[t], 0, 0)),
            pl.BlockSpec((1, D_EXPERT, d), lambda t, rows, nval, e1, e2, nu: (e2[t], 0, 0)),
            pl.BlockSpec((1, d), lambda t, *_: (0, 0)),
            pl.BlockSpec((1, d), lambda t, *_: (0, 0)),
        ],
        out_specs=pl.BlockSpec(memory_space=pl.ANY),
        scratch_shapes=[
            pltpu.VMEM((2, T_MOE, D_ROW), F32),
            pltpu.VMEM((2, T_MOE, d), F32),
            pltpu.SemaphoreType.DMA((2,)),
            pltpu.SemaphoreType.DMA((2,)),
        ],
    )
    return pl.pallas_call(
        _moe_kernel,
        grid_spec=grid_spec,
        out_shape=jax.ShapeDtypeStruct((n_tok, d), F32),
        compiler_params=pltpu.CompilerParams(
            dimension_semantics=("arbitrary",), vmem_limit_bytes=VMEM_LIMIT),
        name="moe",
    )(rows, nval, e1, e2, nused, h1, wgu, wd, wgu, wd, fg, og)


def _rel_bucket(rel):
    half = N_BUCKETS // 2
    max_exact = half // 2
    ret = jnp.where(rel > 0, half, 0)
    n = jnp.abs(rel)
    nf = jnp.maximum(n, 1).astype(F32)
    large = max_exact + (jnp.log(nf / max_exact) / math.log(MAX_DIST / max_exact)
                         * (half - max_exact)).astype(I32)
    large = jnp.minimum(large, half - 1)
    return ret + jnp.where(n < max_exact, n, large)


def _bias_tables(rel_table):
    table = rel_table.astype(F32)
    far = table[_rel_bucket(jnp.array(-8 * MAX_DIST))]
    n_rel = 3 * TQ - 1
    rel = jnp.arange(-(2 * TQ - 1), TQ)
    rev = (table[_rel_bucket(rel)] - far)[::-1].T

    def rows(start, n_rows):
        rolled = jnp.roll(rev, -start, axis=1)
        flat = jnp.tile(rolled, (1, n_rows + 1))[:, :n_rows * (n_rel - 1)]
        return flat.reshape(-1, n_rows, n_rel - 1)[:, :, :TQ]

    prev = rows(n_rel - TQ, TQ)
    diag = rows(TQ - 1, TQ)
    meta0 = rows(TQ + N_META - 1, N_META)
    kk = jnp.arange(TQ)[:, None]
    qq = jnp.arange(TQ)[None, :]
    diag = jnp.where(kk // CHUNK <= qq // CHUNK, diag, NEG_INF)
    masked = jnp.full_like(diag, NEG_INF)
    near = jnp.stack([jnp.concatenate([diag, masked], axis=1),
                      jnp.concatenate([prev, diag], axis=1)], axis=1)
    meta = jnp.stack([meta0, jnp.zeros_like(meta0)], axis=1)
    return near * LOG2E, meta * LOG2E


def _route_tables(cls, n_tiles):
    n = cls.shape[0]
    order = jnp.argsort(cls, stable=True).astype(I32)
    counts = jnp.sum(cls[:, None] == jnp.arange(N_CLASSES)[None, :], axis=0).astype(I32)
    padded = (counts + T_MOE - 1) // T_MOE * T_MOE
    pend = jnp.cumsum(padded)
    pstart = pend - padded
    ustart = jnp.cumsum(counts) - counts
    n_used = pend[-1] // T_MOE
    tile = jnp.arange(n_tiles, dtype=I32)
    tcls = jnp.sum(pend[None, :] <= (tile * T_MOE)[:, None], axis=1).astype(I32)
    tcls = jnp.minimum(tcls, N_CLASSES - 1)
    last = tcls[jnp.maximum(n_used - 1, 0)]
    used = tile < n_used
    tcls = jnp.where(used, tcls, last)
    nval = jnp.where(used, jnp.clip(pstart[tcls] + counts[tcls] - tile * T_MOE, 0, T_MOE), 0).astype(I32)
    r = jnp.arange(T_MOE, dtype=I32)
    src = (ustart[tcls] + tile * T_MOE - pstart[tcls])[:, None] + r[None, :]
    valid = r[None, :] < nval[:, None]
    rows = jnp.where(valid, order[jnp.clip(src, 0, n - 1)], 0).reshape(-1)
    grp = tcls // len(PAIR_LO)
    pair = tcls % len(PAIR_LO)
    e1 = grp * EXPERTS_PER_GROUP + jnp.array(PAIR_LO, I32)[pair]
    e2 = grp * EXPERTS_PER_GROUP + jnp.array(PAIR_HI, I32)[pair]
    return rows, nval, e1.astype(I32), e2.astype(I32), n_used.astype(I32).reshape(1)


def kernel(x, meta_tokens, rel_table, attn_norm, w_in, lambda_q1, lambda_k1, lambda_q2, lambda_k2,
           diff_norm, sb_norm, w_out, ffn_norm, w_group, w_router, w_gate, w_up, w_down, final_norm):
    b, n, d = x.shape
    assert d == D_MODEL and attn_norm.shape[0] == 1
    t_tok = min(T_PROJ, n)
    assert n % t_tok == 0 and n % TQ == 0 and (b * n) % T_MOE == 0

    w = w_in[0]
    scale = HEAD_DIM ** -0.5
    dq, dk, dv, sq, sk, sv = [w[:, i * 512:(i + 1) * 512] for i in range(6)]
    wk = jnp.concatenate([dk, sk], axis=1).astype(BF16)
    wqvt = jnp.concatenate([dq * (scale * LOG2E), dv, sq * scale, sv], axis=1).T.astype(BF16)
    g_attn = attn_norm[0].reshape(1, d)

    kk, qvt = _proj_call(x, g_attn, wk, wqvt, t_tok, TQ)
    meta_pad = jnp.zeros((1, 128, d), x.dtype).at[0, :N_META].set(meta_tokens.astype(x.dtype))
    kk_m, qvt_m = _proj_call(meta_pad, g_attn, wk, wqvt, 128, 128)
    k_meta = kk_m[0, :N_META]
    vt_meta = qvt_m[0, 0, :, :N_META]

    bias_near, bias_meta = _bias_tables(rel_table)
    lamp = jnp.stack([lambda_q1[0], lambda_k1[0], lambda_q2[0], lambda_k2[0]]).astype(F32)
    mixed_a = _diff_call(lamp, kk, qvt, k_meta, vt_meta, bias_near, bias_meta,
                         diff_norm[0].astype(F32).reshape(DIFF_V_DIM, 1))
    mixed_s = _sb_call(kk, qvt, k_meta, vt_meta, sb_norm[0].astype(F32).reshape(HEAD_DIM, 1))

    wo = w_out[0].astype(BF16)
    wr = jnp.concatenate([w_group[0], w_router[0], jnp.zeros((d, 12), F32)], axis=1).T
    wr_hi = wr.astype(BF16)
    wr = jnp.concatenate([wr_hi, (wr - wr_hi.astype(F32)).astype(BF16)], axis=0)
    h1, rc = _outproj_call(x, mixed_a, mixed_s, wo[:512], wo[512:], ffn_norm[0].reshape(1, d), wr, t_tok)

    n_tok = b * n
    n_tiles = n_tok // T_MOE + N_CLASSES
    rows, nval, e1, e2, n_used = _route_tables(rc[:, :, 0, :].reshape(n_tok), n_tiles)

    wgu = jnp.concatenate([w_gate[0], w_up[0]], axis=2).astype(BF16)
    wd = w_down[0].astype(BF16)
    out = _moe_call(rows, nval, e1, e2, n_used, h1.reshape(n_tok, D_ROW), wgu, wd,
                    ffn_norm[0].reshape(1, d), final_norm.reshape(1, d))
    return out.reshape(b, n, d)
```
